```python
import jax
import jax.numpy as jnp
from jax import lax
import numpy as np

D_MODEL = 1024
BATCH = 16
SEQ = 2048
DEPTH = 2
DEC_BATCH = 128
DEC_SEQ = 8
PAST_LEN = 16384
PAGE_SIZE = 128

H_A = 8
DN_A = 64
DR_A = 32
DV_A = 64
R_Q = 256
R_KV = 128
SCALE_A = (DN_A + DR_A) ** -0.5
H_B = 8
KVH_B = 2
HD_B = 64
ROT_B = HD_B // 4
HI_B = 4
DI_B = 64
ROT_I = DI_B // 4
TOPK_MAX = 256
H_C = 4
DK_C = 64
DV_C = 128
R_GATE_C = 16
GLA_TAU = 16.0
GLA_CHUNK = 16
D_FF = 4 * D_MODEL
D_PLE = 256
ROPE_THETA = 500000.0
EPS = 1e-6
Q_BLOCK = 128
POOL_NUM = 5
POOL_DEN = 4

SPLITS = (
    ('a_q', R_Q), ('a_kv', R_KV), ('a_kr', DR_A),
    ('b_q', H_B * HD_B), ('b_k', KVH_B * HD_B), ('b_v', KVH_B * HD_B),
    ('b_iq', HI_B * DI_B), ('b_ik', DI_B), ('b_iw', HI_B),
    ('c_q', H_C * DK_C), ('c_k', H_C * DK_C), ('c_v', H_C * DV_C), ('c_r', H_C * DV_C), ('c_a', R_GATE_C),
    ('g_a', D_MODEL), ('g_b', D_MODEL), ('g_c', D_MODEL),
)
D_IN = sum(n for _, n in SPLITS)
F32 = jnp.float32

kernel_name = 'hybrid_mla_dsa_gla_step'


def rmsnorm(x, g):
    xf = x.astype(F32)
    y = xf * lax.rsqrt(jnp.mean(xf * xf, axis=-1, keepdims=True) + EPS)
    return (y * g.astype(F32)).astype(x.dtype)


def rope_tables(pos, rot):
    inv = ROPE_THETA ** (-jnp.arange(0, rot, 2, dtype=F32) / rot)
    ang = pos.astype(F32)[:, None] * inv[None, :]
    return jnp.cos(ang), jnp.sin(ang)


def apply_rope(x, cos, sin):
    half = cos.shape[-1]
    shp = (cos.shape[0],) + (1,) * (x.ndim - 3) + (half,)
    c, s = cos.reshape(shp), sin.reshape(shp)
    xf = x.astype(F32)
    x1, x2, rest = xf[..., :half], xf[..., half:2 * half], xf[..., 2 * half:]
    return jnp.concatenate([x1 * c - x2 * s, x2 * c + x1 * s, rest], axis=-1).astype(x.dtype)


def split_cols(z):
    out, off = {}, 0
    for name, n in SPLITS:
        out[name] = z[..., off:off + n]
        off += n
    return out


def over_query_blocks(fn, *arrs):
    b, t = arrs[0].shape[:2]
    nb = t // Q_BLOCK
    blocks = tuple(a.reshape((b, nb, Q_BLOCK) + a.shape[2:]).swapaxes(0, 1) for a in arrs)
    out = lax.map(lambda xs: fn(xs[0] * Q_BLOCK, *xs[1:]), (jnp.arange(nb),) + blocks)
    return out.swapaxes(0, 1).reshape((b, t) + out.shape[3:])


def mla_keys(c, kr, w_uk):
    kn = jnp.einsum('bsr,rhd->bshd', c, w_uk)
    ms = (jnp.sum(jnp.square(kn.astype(F32)), -1)
          + jnp.sum(jnp.square(kr.astype(F32)), -1)[..., None]) / (DN_A + DR_A)
    return kn, jnp.swapaxes(lax.rsqrt(ms + EPS), 1, 2)


def mla_attend(qn, qr, kn, kinv, c, kr, mask, w_uv):
    s = (jnp.einsum('bthd,bshd->bhts', qn, kn).astype(F32)
         + jnp.einsum('bthd,bsd->bhts', qr, kr).astype(F32))
    s = jnp.where(mask, s * kinv[:, :, None, :] * SCALE_A, -jnp.inf)
    p = jax.nn.softmax(s, axis=-1).astype(c.dtype)
    o_lat = jnp.einsum('bhts,bsr->bthr', p, c)
    return jnp.einsum('bthr,rhd->bthd', o_lat, w_uv)


def indexer_scores(iq, iw, ik, mask):
    dots = jax.nn.relu(jnp.einsum('bthd,bsd->bths', iq, ik).astype(F32) * DI_B ** -0.5)
    sc = jnp.einsum('bths,bth->bts', dots, iw.astype(F32))
    return jnp.where(mask, sc, -jnp.inf)


def dsa_attend(q, k_sel, v_sel, valid):
    b, t = q.shape[:2]
    qg = q.reshape(b, t, KVH_B, H_B // KVH_B, HD_B)
    s = jnp.einsum('btngd,btjnd->btngj', qg, k_sel).astype(F32) * HD_B ** -0.5
    s = jnp.where(valid[:, :, None, None, :], s, -jnp.inf)
    p = jax.nn.softmax(s, axis=-1).astype(v_sel.dtype)
    return jnp.einsum('btngj,btjnd->btngd', p, v_sel).reshape(b, t, H_B * HD_B)


def gla_chunk(s0, q, k, v, la):
    qf, kf, vf = q.astype(F32), k.astype(F32), v.astype(F32)
    lb = jnp.cumsum(la, axis=1)
    c = q.shape[1]
    causal = jnp.tril(jnp.ones((c, c), bool))[None, :, :, None, None]
    dec = jnp.exp(jnp.where(causal, lb[:, :, None] - lb[:, None, :], -jnp.inf))
    att = jnp.einsum('bthk,bshk,btshk->bhts', qf, kf, dec)
    o = (jnp.einsum('bthk,bhkv->bthv', qf * jnp.exp(lb), s0)
         + jnp.einsum('bhts,bshv->bthv', att, vf))
    lb_last = lb[:, -1]
    s_new = (s0 * jnp.exp(lb_last)[..., None]
             + jnp.einsum('bshk,bshv->bhkv', kf * jnp.exp(lb_last[:, None] - lb), vf))
    return s_new, o


def front(x, lw, pos):
    b, t, _ = x.shape
    z = split_cols(rmsnorm(x, lw['g_attn']) @ lw['w_in'])
    ca, sa = rope_tables(pos, DR_A)
    cb, sb = rope_tables(pos, ROT_B)
    ci, si = rope_tables(pos, ROT_I)
    qa = (rmsnorm(z['a_q'], lw['g_qa']) @ lw['w_uq']).reshape(b, t, H_A, DN_A + DR_A)
    qa = rmsnorm(qa, lw['g_q_a'])
    m_a = dict(qn=qa[..., :DN_A], qr=apply_rope(qa[..., DN_A:], ca, sa),
               c=rmsnorm(z['a_kv'], lw['g_kva']), kr=apply_rope(z['a_kr'], ca, sa))
    m_b = dict(q=apply_rope(rmsnorm(z['b_q'].reshape(b, t, H_B, HD_B), lw['g_q_b']), cb, sb),
               k=apply_rope(rmsnorm(z['b_k'].reshape(b, t, KVH_B, HD_B), lw['g_k_b']), cb, sb),
               v=z['b_v'].reshape(b, t, KVH_B, HD_B),
               iq=apply_rope(z['b_iq'].reshape(b, t, HI_B, DI_B), ci, si),
               ik=apply_rope(z['b_ik'], ci, si),
               iw=z['b_iw'] * HI_B ** -0.5)
    la = jax.nn.log_sigmoid((z['c_a'] @ lw['w_a2'] + lw['b_a']).astype(F32)) / GLA_TAU
    m_c = dict(q=z['c_q'].reshape(b, t, H_C, DK_C),
               k=z['c_k'].reshape(b, t, H_C, DK_C) * DK_C ** -0.5,
               v=z['c_v'].reshape(b, t, H_C, DV_C),
               la=la.reshape(b, t, H_C, DK_C))
    return z, m_a, m_b, m_c


def back(x, z, oa, ob, oc, p_l, lw):
    b, t, _ = x.shape
    oc = rmsnorm(oc, lw['g_o_c']).astype(x.dtype) * jax.nn.silu(z['c_r']).reshape(b, t, H_C, DV_C)
    merged = (jax.nn.sigmoid(z['g_a']) * (oa.reshape(b, t, -1) @ lw['w_pa'])
              + jax.nn.sigmoid(z['g_b']) * (ob @ lw['w_pb'])
              + jax.nn.sigmoid(z['g_c']) * (oc.reshape(b, t, -1) @ lw['w_pc']))
    x = x + merged @ lw['w_o']
    x = x + jnp.square(jax.nn.relu(rmsnorm(x, lw['g_ffn']) @ lw['w_up'])) @ lw['w_down']
    gate = jax.nn.sigmoid(rmsnorm(x, lw['g_ple']) @ lw['w_pg'])
    return x + gate * (p_l @ lw['w_pe'])


def prompt_layer(x, p_l, lw):
    b, t, _ = x.shape
    pos = jnp.arange(t)
    z, m_a, m_b, m_c = front(x, lw, pos)
    kn, kinv = mla_keys(m_a['c'], m_a['kr'], lw['w_uk'])

    def a_blk(start, qn, qr):
        mask = pos[None, :] <= (start + jnp.arange(Q_BLOCK))[:, None]
        return mla_attend(qn, qr, kn, kinv, m_a['c'], m_a['kr'], mask, lw['w_uv'])
    oa = over_query_blocks(a_blk, m_a['qn'], m_a['qr'])
    n_sel = min(TOPK_MAX, t // 4)
    take = jax.vmap(lambda rows, idx: rows[idx])

    def b_blk(start, q, iq, iw):
        qpos = start + jnp.arange(Q_BLOCK)
        sc = indexer_scores(iq, iw, m_b['ik'], pos[None, :] <= qpos[:, None])
        idx = lax.top_k(sc, n_sel)[1]
        return dsa_attend(q, take(m_b['k'], idx), take(m_b['v'], idx), idx <= qpos[None, :, None])
    ob = over_query_blocks(b_blk, m_b['q'], m_b['iq'], m_b['iw'])
    nc = t // GLA_CHUNK
    chunks = lambda a: a.reshape((b, nc, GLA_CHUNK) + a.shape[2:]).swapaxes(0, 1)
    s0 = jnp.zeros((b, H_C, DK_C, DV_C), F32)
    s_fin, oc = lax.scan(lambda s, xs: gla_chunk(s, *xs), s0,
                         (chunks(m_c['q']), chunks(m_c['k']), chunks(m_c['v']), chunks(m_c['la'])))
    oc = oc.swapaxes(0, 1).reshape(b, t, H_C, DV_C)
    y = back(x, z, oa, ob, oc, p_l, lw)
    return y, (m_a['c'], m_a['kr'], m_b['k'], m_b['v'], m_b['ik'], s_fin.astype(x.dtype))


def sample_layer(x, p_l, lw, l, cache_ckv, cache_krope, cache_k, cache_v, cache_kidx, s_prev, page_table):
    b, t, _ = x.shape
    past = page_table.shape[1] * PAGE_SIZE
    pos = past + jnp.arange(t)
    z, m_a, m_b, m_c = front(x, lw, pos)
    mask = jnp.arange(past + t)[None, :] <= pos[:, None]

    def a_seq(xs):
        qn, qr, c, kr, pt = xs
        c_all = jnp.concatenate([cache_ckv[l, pt].reshape(past, R_KV), c], 0)[None]
        kr_all = jnp.concatenate([cache_krope[l, pt].reshape(past, DR_A), kr], 0)[None]
        kn, kinv = mla_keys(c_all, kr_all, lw['w_uk'])
        return mla_attend(qn[None], qr[None], kn, kinv, c_all, kr_all, mask, lw['w_uv'])[0]
    oa = lax.map(a_seq, (m_a['qn'], m_a['qr'], m_a['c'], m_a['kr'], page_table))

    n_sel = min(TOPK_MAX, (past + t) // 4)
    ik_all = jnp.concatenate([cache_kidx[l, page_table].reshape(b, past, DI_B), m_b['ik']], 1)
    idx = lax.top_k(indexer_scores(m_b['iq'], m_b['iw'], ik_all, mask), n_sel)[1]
    in_past = (idx < past)[..., None, None]
    i_past = jnp.minimum(idx, past - 1)
    phys = jax.vmap(lambda pt, i: pt[i])(page_table, i_past // PAGE_SIZE)
    off = i_past % PAGE_SIZE
    i_new = jnp.clip(idx - past, 0, t - 1)
    take = jax.vmap(lambda rows, i: rows[i])
    k_sel = jnp.where(in_past, cache_k[l, phys, off], take(m_b['k'], i_new))
    v_sel = jnp.where(in_past, cache_v[l, phys, off], take(m_b['v'], i_new))
    ob = dsa_attend(m_b['q'], k_sel, v_sel, idx <= pos[None, :, None])

    s_new, oc = gla_chunk(s_prev.astype(F32), m_c['q'], m_c['k'], m_c['v'], m_c['la'])
    y = back(x, z, oa, ob, oc, p_l, lw)
    return y, (m_a['c'], m_a['kr'], m_b['k'], m_b['v'], m_b['ik'], s_new.astype(x.dtype))


def _stack(rows, i):
    return jnp.stack([r[i] for r in rows], axis=0)


def setup_inputs(seed: int = 0) -> dict:
    key = jax.random.key(seed)
    ks = jax.random.split(key, 34)
    nrm = lambda i, shape, scale: jax.random.normal(ks[i], shape, F32) * scale
    gain = lambda i, shape: 1.0 + nrm(i, shape, 0.05)
    n_pages = PAST_LEN // PAGE_SIZE
    n_used = DEC_BATCH * n_pages
    n_phys = (n_used * POOL_NUM + POOL_DEN - 1) // POOL_DEN
    page_table = jax.random.permutation(ks[8], n_phys)[:n_used].reshape(DEC_BATCH, n_pages).astype(jnp.int32)
    return {
        'x_prompt': nrm(0, (BATCH, SEQ, D_MODEL), 1.0),
        'x_sample': nrm(1, (DEC_BATCH, DEC_SEQ, D_MODEL), 1.0),
        'cache_ckv': nrm(2, (DEPTH, n_phys, PAGE_SIZE, R_KV), 1.0),
        'cache_krope': nrm(3, (DEPTH, n_phys, PAGE_SIZE, DR_A), 1.0),
        'cache_k': nrm(4, (DEPTH, n_phys, PAGE_SIZE, KVH_B, HD_B), 1.0),
        'cache_v': nrm(5, (DEPTH, n_phys, PAGE_SIZE, KVH_B, HD_B), 1.0),
        'cache_kidx': nrm(6, (DEPTH, n_phys, PAGE_SIZE, DI_B), 1.0),
        'state_gla': nrm(7, (DEPTH, DEC_BATCH, H_C, DK_C, DV_C), 0.5),
        'page_table': page_table,
        'p_prompt': nrm(9, (DEPTH, BATCH, SEQ, D_PLE), 1.0),
        'p_sample': nrm(10, (DEPTH, DEC_BATCH, DEC_SEQ, D_PLE), 1.0),
        'g_attn': gain(11, (DEPTH, D_MODEL)),
        'w_in': nrm(12, (DEPTH, D_MODEL, D_IN), D_MODEL ** -0.5),
        'g_qa': gain(13, (DEPTH, R_Q)),
        'w_uq': nrm(14, (DEPTH, R_Q, H_A * (DN_A + DR_A)), R_Q ** -0.5),
        'g_q_a': gain(15, (DEPTH, DN_A + DR_A)),
        'g_kva': gain(16, (DEPTH, R_KV)),
        'w_uk': nrm(17, (DEPTH, R_KV, H_A, DN_A), R_KV ** -0.5),
        'w_uv': nrm(18, (DEPTH, R_KV, H_A, DV_A), R_KV ** -0.5),
        'g_q_b': gain(19, (DEPTH, HD_B)),
        'g_k_b': gain(20, (DEPTH, HD_B)),
        'w_a2': nrm(21, (DEPTH, R_GATE_C, H_C * DK_C), R_GATE_C ** -0.5),
        'b_a': nrm(22, (DEPTH, H_C * DK_C), 0.1),
        'g_o_c': gain(23, (DEPTH, DV_C)),
        'w_pa': nrm(24, (DEPTH, H_A * DV_A, D_MODEL), (H_A * DV_A) ** -0.5),
        'w_pb': nrm(25, (DEPTH, H_B * HD_B, D_MODEL), (H_B * HD_B) ** -0.5),
        'w_pc': nrm(26, (DEPTH, H_C * DV_C, D_MODEL), (H_C * DV_C) ** -0.5),
        'w_o': nrm(27, (DEPTH, D_MODEL, D_MODEL), D_MODEL ** -0.5),
        'g_ffn': gain(28, (DEPTH, D_MODEL)),
        'w_up': nrm(29, (DEPTH, D_MODEL, D_FF), D_MODEL ** -0.5),
        'w_down': nrm(30, (DEPTH, D_FF, D_MODEL), D_FF ** -0.5),
        'g_ple': gain(31, (DEPTH, D_MODEL)),
        'w_pe': nrm(32, (DEPTH, D_PLE, D_MODEL), D_PLE ** -0.5),
        'w_pg': nrm(33, (DEPTH, D_MODEL, D_MODEL), D_MODEL ** -0.5),
    }


def reference(x_prompt, x_sample, cache_ckv, cache_krope, cache_k, cache_v, cache_kidx, state_gla, page_table,
              p_prompt, p_sample, g_attn, w_in, g_qa, w_uq, g_q_a, g_kva, w_uk, w_uv, g_q_b, g_k_b, w_a2, b_a,
              g_o_c, w_pa, w_pb, w_pc, w_o, g_ffn, w_up, w_down, g_ple, w_pe, w_pg):
    xp, xs = x_prompt, x_sample
    new_p, new_s = [], []
    for l in range(DEPTH):
        lw = dict(g_attn=g_attn[l], w_in=w_in[l], g_qa=g_qa[l], w_uq=w_uq[l], g_q_a=g_q_a[l], g_kva=g_kva[l],
                  w_uk=w_uk[l], w_uv=w_uv[l], g_q_b=g_q_b[l], g_k_b=g_k_b[l], w_a2=w_a2[l], b_a=b_a[l],
                  g_o_c=g_o_c[l], w_pa=w_pa[l], w_pb=w_pb[l], w_pc=w_pc[l], w_o=w_o[l], g_ffn=g_ffn[l],
                  w_up=w_up[l], w_down=w_down[l], g_ple=g_ple[l], w_pe=w_pe[l], w_pg=w_pg[l])
        xp, rows_p = prompt_layer(xp, p_prompt[l], lw)
        xs, rows_s = sample_layer(xs, p_sample[l], lw, l, cache_ckv, cache_krope, cache_k, cache_v, cache_kidx,
                                  state_gla[l], page_table)
        new_p.append(rows_p)
        new_s.append(rows_s)
    return (xp, xs,
            _stack(new_p, 0), _stack(new_p, 1), _stack(new_p, 2), _stack(new_p, 3), _stack(new_p, 4), _stack(new_p, 5),
            _stack(new_s, 0), _stack(new_s, 1), _stack(new_s, 2), _stack(new_s, 3), _stack(new_s, 4), _stack(new_s, 5))
```

```python
import functools

import numpy as np
import jax
import jax.numpy as jnp
from jax import lax
from jax.experimental import pallas as pl
from jax.experimental.pallas import tpu as pltpu

D_MODEL = 1024
PAGE = 128
H_A, DN_A, DR_A, DV_A, R_Q, R_KV = 8, 64, 32, 64, 256, 128
SCALE_A = (DN_A + DR_A) ** -0.5
H_B, KVH_B, HD_B = 8, 2, 64
ROT_B = HD_B // 4
HI_B, DI_B = 4, 64
TOPK_MAX = 256
H_C, DK_C, DV_C, R_GATE_C = 4, 64, 128, 16
GLA_TAU = 16.0
GLA_CHUNK = 16
D_FF = 4 * D_MODEL
D_PLE = 256
ROPE_THETA = 500000.0
EPS = 1e-6

N_A = R_Q + R_KV + DR_A
N_B = H_B * HD_B + 2 * KVH_B * HD_B + HI_B * DI_B + DI_B + HI_B
N_C = 2 * H_C * DK_C + 2 * H_C * DV_C + R_GATE_C
W_A, W_B, W_C = 512, 1152, 1664
LANE = 128
NEG = -1e30
INT_MIN = -(2 ** 31)

F32 = jnp.float32
BF16 = jnp.bfloat16
VMEM_LIMIT = 48 * 1024 * 1024


def _dot(a, b):
    return jnp.dot(a, b, preferred_element_type=F32)


def _dot_nt(a, b):
    return lax.dot_general(a, b, (((1,), (1,)), ((), ())), preferred_element_type=F32)


def _dot_tn(a, b):
    return lax.dot_general(a, b, (((0,), (0,)), ((), ())), preferred_element_type=F32)


def _split3(a):
    a1 = a.astype(BF16)
    r1 = a - a1.astype(F32)
    a2 = r1.astype(BF16)
    a3 = (r1 - a2.astype(F32)).astype(BF16)
    return a1, a2, a3


def _dot_f32(a, b_bf):
    a1, a2, a3 = _split3(a)
    return _dot(a1, b_bf) + _dot(a2, b_bf) + _dot(a3, b_bf)


def _dot_nt_f32(a_bf, b):
    b1, b2, b3 = _split3(b)
    return _dot_nt(a_bf, b1) + _dot_nt(a_bf, b2) + _dot_nt(a_bf, b3)


def _rms(x, g):
    return x * lax.rsqrt(jnp.mean(x * x, axis=-1, keepdims=True) + EPS) * g


def _seg_inv(x, e, et, n):
    inv = lax.rsqrt(_dot_f32(x * x, e) * (1.0 / n) + EPS)
    return _dot_f32(inv, et)


def _rope(x, cos, sin, period, half):
    w = x.shape[-1]
    reps = w // LANE
    if reps > 1:
        cos = jnp.concatenate([cos] * reps, axis=1)
        sin = jnp.concatenate([sin] * reps, axis=1)
    lane = lax.broadcasted_iota(jnp.int32, x.shape, 1)
    first = (lane % period) < half
    partner = jnp.where(first, pltpu.roll(x, w - half, 1), pltpu.roll(x, half, 1))
    return x * cos + partner * sin


def _sigmoid(x):
    return 1.0 / (1.0 + jnp.exp(-x))


def _front_ab_kernel(x_ref, g_ref, wa_ref, wb_ref, gqa_ref, wq_ref, gn_ref, gr_ref, gkv_ref, wuk_ref,
                     wabs_ref, gqb_ref, gkb_ref, ca_ref, sa_ref, cb_ref, sb_ref,
                     e64_ref, e64t_ref, e128_ref, e128t_ref, e8_ref,
                     q_ref, kcat_ref, c_ref, kr_ref, kinv_ref,
                     bq_ref, bk_ref, bkb_ref, bv_ref, bvb_ref, iq_ref, ik_ref, ikb_ref, iw_ref):
    xn = _rms(x_ref[...], g_ref[...]).astype(BF16)
    za = _dot(xn, wa_ref[...])
    zb = _dot(xn, wb_ref[...])
    ca, sa, cb, sb = ca_ref[...], sa_ref[...], cb_ref[...], sb_ref[...]
    e64, e64t = e64_ref[...], e64t_ref[...]

    aq = _rms(za[:, :R_Q], gqa_ref[...]).astype(BF16)
    q2 = _dot(aq, wq_ref[...])
    qn, qr = q2[:, :512], q2[:, 512:]
    ss = _dot_f32(qn * qn, e64) + _dot_f32(qr * qr, e128_ref[...])
    inv = lax.rsqrt(ss * (1.0 / (DN_A + DR_A)) + EPS)
    qn = qn * _dot_f32(inv, e64t) * gn_ref[...]
    qr = qr * _dot_f32(inv, e128t_ref[...]) * gr_ref[...]
    qr = _rope(qr, ca, sa, LANE, DR_A // 2) * SCALE_A
    qlat = _dot(qn.astype(BF16), wabs_ref[...])
    for h in range(H_A):
        q_ref[:, h * 256:h * 256 + 128] = qlat[:, h * 128:(h + 1) * 128].astype(BF16)
        q_ref[:, h * 256 + 128:(h + 1) * 256] = qr[:, h * 128:(h + 1) * 128].astype(BF16)

    c = _rms(za[:, R_Q:R_Q + R_KV], gkv_ref[...])
    kr = _rope(za[:, R_Q + R_KV:], ca, sa, LANE, DR_A // 2)
    c_ref[...] = c
    kr_ref[...] = kr[:, :DR_A]
    cb16 = c.astype(BF16)
    kcat_ref[:, :R_KV] = cb16
    kcat_ref[:, R_KV:] = kr.astype(BF16)
    kn = _dot(cb16, wuk_ref[...])
    e8 = e8_ref[...]
    ones8 = jnp.ones((8, LANE), BF16)
    ms = _dot_nt_f32(e8, kn * kn) + _dot_nt_f32(ones8, kr * kr)
    kinv_ref[...] = lax.rsqrt(ms * (1.0 / (DN_A + DR_A)) + EPS)

    bq = zb[:, :512]
    bq = bq * _seg_inv(bq, e64, e64t, HD_B) * gqb_ref[...]
    bq_ref[...] = (_rope(bq, cb, sb, HD_B, ROT_B // 2) * (HD_B ** -0.5)).astype(BF16)
    bk = zb[:, 512:640]
    bk = bk * _seg_inv(bk, e64[:128], e64t[:, :128], HD_B) * gkb_ref[...]
    bk = _rope(bk, cb, sb, HD_B, ROT_B // 2)
    bk_ref[...] = bk
    bkb_ref[...] = bk.astype(BF16)
    bv = zb[:, 640:768]
    bv_ref[...] = bv
    bvb_ref[...] = bv.astype(BF16)
    iq_ref[...] = (_rope(zb[:, 768:1024], cb, sb, DI_B, ROT_B // 2) * (DI_B ** -0.5)).astype(BF16)
    last = zb[:, 1024:1152]
    ik = _rope(last, cb, sb, DI_B, ROT_B // 2)[:, :DI_B]
    ik_ref[...] = ik
    ikb_ref[...] = ik.astype(BF16)
    iw_ref[...] = last * (HI_B ** -0.5)


def _front_ab(x2d, tabs, lw, tm):
    n = x2d.shape[0]
    nblk = tabs[0].shape[0] // tm
    row = lambda w: pl.BlockSpec((tm, w), lambda i: (i, 0))
    full = lambda a: pl.BlockSpec(a.shape, lambda i: (0,) * a.ndim)
    tab = pl.BlockSpec((tm, LANE), lambda i: (i % nblk, 0))
    consts = [lw['g_attn'], lw['wa'], lw['wb'], lw['g_qa'], lw['wq2'], lw['gn'], lw['gr'], lw['g_kva'],
              lw['wuk'], lw['wabs'], lw['gqb'], lw['gkb']]
    inds = [lw['e64'], lw['e64t'], lw['e128'], lw['e128t'], lw['e8']]
    outs = [(2048, BF16), (256, BF16), (R_KV, F32), (DR_A, F32), None,
            (512, BF16), (128, F32), (128, BF16), (128, F32), (128, BF16),
            (256, BF16), (DI_B, F32), (DI_B, BF16), (128, F32)]
    out_shape, out_specs = [], []
    for o in outs:
        if o is None:
            out_shape.append(jax.ShapeDtypeStruct((8, n), F32))
            out_specs.append(pl.BlockSpec((8, tm), lambda i: (0, i)))
        else:
            out_shape.append(jax.ShapeDtypeStruct((n, o[0]), o[1]))
            out_specs.append(row(o[0]))
    return pl.pallas_call(
        _front_ab_kernel, grid=(n // tm,),
        in_specs=[row(D_MODEL)] + [full(a) for a in consts] + [tab] * 4 + [full(a) for a in inds],
        out_specs=out_specs, out_shape=out_shape, name='front_ab',
        compiler_params=pltpu.CompilerParams(dimension_semantics=('arbitrary',), vmem_limit_bytes=VMEM_LIMIT),
    )(x2d, *consts, *tabs, *inds)


def _front_cg_kernel(x_ref, g_ref, wc_ref, wg_ref, wa2_ref, ba_ref,
                     cq_ref, ck_ref, cv_ref, la_ref, cr_ref, gate_ref):
    xn = _rms(x_ref[...], g_ref[...]).astype(BF16)
    zc = _dot(xn, wc_ref[...])
    cq_ref[...] = zc[:, :256]
    ck_ref[...] = zc[:, 256:512] * (DK_C ** -0.5)
    cv_ref[...] = zc[:, 512:1024]
    cr_ref[...] = zc[:, 1024:1536]
    u = _dot(zc[:, 1536:1664].astype(BF16), wa2_ref[...]) + ba_ref[...]
    la_ref[...] = (jnp.minimum(u, 0.0) - jnp.log1p(jnp.exp(-jnp.abs(u)))) * (1.0 / GLA_TAU)
    gate_ref[...] = _dot(xn, wg_ref[...])


def _front_cg(x2d, lw, tm):
    n = x2d.shape[0]
    row = lambda w: pl.BlockSpec((tm, w), lambda i: (i, 0))
    full = lambda a: pl.BlockSpec(a.shape, lambda i: (0,) * a.ndim)
    consts = [lw['g_attn'], lw['wc'], lw['wg'], lw['wa2'], lw['b_a']]
    widths = [256, 256, 512, 256, 512, 3 * D_MODEL]
    return pl.pallas_call(
        _front_cg_kernel, grid=(n // tm,),
        in_specs=[row(D_MODEL)] + [full(a) for a in consts],
        out_specs=[row(w) for w in widths],
        out_shape=[jax.ShapeDtypeStruct((n, w), F32) for w in widths], name='front_cg',
        compiler_params=pltpu.CompilerParams(dimension_semantics=('arbitrary',), vmem_limit_bytes=VMEM_LIMIT),
    )(x2d, *consts)


def _mla_prompt_kernel(q_ref, k_ref, kinv_ref, wuv_ref, o_ref, m_ref, l_ref, acc_ref, *, tq, tk):
    i, j = pl.program_id(1), pl.program_id(2)

    @pl.when(j == 0)
    def _():
        m_ref[...] = jnp.full(m_ref.shape, NEG, F32)
        l_ref[...] = jnp.zeros(l_ref.shape, F32)
        acc_ref[...] = jnp.zeros(acc_ref.shape, F32)

    @pl.when(j <= i)
    def _():
        k = k_ref[...]
        v = k[:, :R_KV]
        kinv = kinv_ref[...]
        rowp = i * tq + lax.broadcasted_iota(jnp.int32, (tq, tk), 0)
        colp = j * tk + lax.broadcasted_iota(jnp.int32, (tq, tk), 1)
        mask = colp <= rowp
        for h in range(H_A):
            s = _dot_nt(q_ref[:, h * 256:(h + 1) * 256], k) * kinv[h:h + 1, :]
            s = jnp.where(mask, s, NEG)
            m_prev = m_ref[h]
            m_new = jnp.maximum(m_prev, jnp.max(s, axis=-1, keepdims=True))
            p = jnp.exp(s - m_new)
            alpha = jnp.exp(m_prev - m_new)
            l_ref[h] = alpha * l_ref[h] + jnp.sum(p, axis=-1, keepdims=True)
            acc_ref[h] = alpha * acc_ref[h] + _dot(p.astype(BF16), v)
            m_ref[h] = m_new

    @pl.when(j == pl.num_programs(2) - 1)
    def _():
        olat = jnp.concatenate([(acc_ref[h] / l_ref[h]).astype(BF16) for h in range(H_A)], axis=1)
        o_ref[...] = _dot(olat, wuv_ref[...]).astype(BF16)


def _mla_prompt(q, kcat, kinv, wuv_bd, b, t):
    tq = tk = min(256, t)
    nq = t // tq
    return pl.pallas_call(
        functools.partial(_mla_prompt_kernel, tq=tq, tk=tk), grid=(b, nq, nq),
        in_specs=[pl.BlockSpec((tq, 2048), lambda bb, i, j: (bb * nq + i, 0)),
                  pl.BlockSpec((tk, 256), lambda bb, i, j: (bb * nq + jnp.minimum(j, i), 0)),
                  pl.BlockSpec((8, tk), lambda bb, i, j: (0, bb * nq + jnp.minimum(j, i))),
                  pl.BlockSpec(wuv_bd.shape, lambda bb, i, j: (0, 0))],
        out_specs=pl.BlockSpec((tq, 512), lambda bb, i, j: (bb * nq + i, 0)),
        out_shape=jax.ShapeDtypeStruct((b * t, 512), BF16),
        scratch_shapes=[pltpu.VMEM((H_A, tq, 1), F32), pltpu.VMEM((H_A, tq, 1), F32),
                        pltpu.VMEM((H_A, tq, R_KV), F32)], name='mla_prompt',
        compiler_params=pltpu.CompilerParams(dimension_semantics=('arbitrary', 'arbitrary', 'arbitrary'),
                                             vmem_limit_bytes=VMEM_LIMIT),
    )(q, kcat, kinv, wuv_bd)


def _score_key(sc):
    sc = jnp.where(sc == 0.0, 0.0, sc)
    bits = lax.bitcast_convert_type(sc, jnp.int32)
    return jnp.where(bits < 0, bits ^ 0x7FFFFFFF, bits)


def _kth_largest(count_ge, k, rows):
    base = jnp.where(count_ge(jnp.zeros((rows, 1), jnp.int32)) >= k, 0, INT_MIN).astype(jnp.int32)

    def body(it, base):
        cand = base + jnp.left_shift(jnp.int32(1), 30 - it)
        return jnp.where(count_ge(cand) >= k, cand, base)
    return lax.fori_loop(0, 31, body, base)


def _cumsum_lanes(x):
    w = x.shape[-1]
    lane = lax.broadcasted_iota(jnp.int32, x.shape, x.ndim - 1)
    sh = 1
    while sh < w:
        x = x + jnp.where(lane >= sh, pltpu.roll(x, sh, x.ndim - 1), 0.0)
        sh *= 2
    return x


def _select(keys, kth, need, carry):
    eq = (keys == kth).astype(F32)
    pref = _cumsum_lanes(eq) - eq + carry
    sel = (keys > kth) | ((eq > 0.0) & (pref < need))
    return sel, carry + jnp.sum(eq, axis=-1, keepdims=True)


def _dsa_prompt_kernel(bq_ref, iq_ref, iw_ref, ik_ref, k_ref, v_ref, o_ref,
                       keys_ref, m_ref, l_ref, acc_ref, *, tq, tkb, n_sel):
    i = pl.program_id(1)
    nkb = (i * tq + tq + tkb - 1) // tkb
    rowp = i * tq + lax.broadcasted_iota(jnp.int32, (tq, tkb), 0)
    col0 = lax.broadcasted_iota(jnp.int32, (tq, tkb), 1)
    iw = iw_ref[...]

    def score_body(kb, carry):
        off = pl.multiple_of(kb * tkb, tkb)
        ik = ik_ref[pl.ds(off, tkb), :]
        sc = jnp.zeros((tq, tkb), F32)
        for h in range(HI_B):
            d = _dot_nt(iq_ref[:, h * DI_B:(h + 1) * DI_B], ik)
            sc = sc + jnp.maximum(d, 0.0) * iw[:, DI_B + h:DI_B + h + 1]
        sc = jnp.where(col0 + off <= rowp, sc, -jnp.inf)
        keys_ref[:, pl.ds(off, tkb)] = _score_key(sc)
        return carry
    lax.fori_loop(0, nkb, score_body, 0)

    def count(pred):
        def body(kb, acc):
            off = pl.multiple_of(kb * tkb, tkb)
            return acc + jnp.sum(jnp.where(pred(keys_ref[:, pl.ds(off, tkb)]), 1.0, 0.0), axis=-1, keepdims=True)
        return lax.fori_loop(0, nkb, body, jnp.zeros((tq, 1), F32))

    kth = _kth_largest(lambda cand: count(lambda blk: blk >= cand), float(n_sel), tq)
    need = float(n_sel) - count(lambda blk: blk > kth)

    m_ref[...] = jnp.full(m_ref.shape, NEG, F32)
    l_ref[...] = jnp.zeros(l_ref.shape, F32)
    acc_ref[...] = jnp.zeros(acc_ref.shape, F32)

    def attn_body(kb, carry):
        off = pl.multiple_of(kb * tkb, tkb)
        sel, carry = _select(keys_ref[:, pl.ds(off, tkb)], kth, need, carry)
        sel = sel & (col0 + off <= rowp)
        k = k_ref[pl.ds(off, tkb), :]
        v = v_ref[pl.ds(off, tkb), :]
        for h in range(H_B):
            n = h // (H_B // KVH_B)
            s = _dot_nt(bq_ref[:, h * HD_B:(h + 1) * HD_B], k[:, n * HD_B:(n + 1) * HD_B])
            s = jnp.where(sel, s, NEG)
            m_prev = m_ref[h]
            m_new = jnp.maximum(m_prev, jnp.max(s, axis=-1, keepdims=True))
            p = jnp.where(sel, jnp.exp(s - m_new), 0.0)
            alpha = jnp.exp(m_prev - m_new)
            l_ref[h] = alpha * l_ref[h] + jnp.sum(p, axis=-1, keepdims=True)
            acc_ref[h] = alpha * acc_ref[h] + _dot(p.astype(BF16), v[:, n * HD_B:(n + 1) * HD_B])
            m_ref[h] = m_new
        return carry
    lax.fori_loop(0, nkb, attn_body, jnp.zeros((tq, 1), F32))
    o_ref[...] = jnp.concatenate([acc_ref[h] / l_ref[h] for h in range(H_B)], axis=1).astype(BF16)


def _dsa_prompt(bq, iq, iw, ikb, bkb, bvb, b, t):
    tq = min(128, t)
    tkb = min(512, t)
    nq = t // tq
    n_sel = min(TOPK_MAX, t // 4)
    qrow = lambda w: pl.BlockSpec((tq, w), lambda bb, i: (bb * nq + i, 0))
    seq = lambda w: pl.BlockSpec((t, w), lambda bb, i: (bb, 0))
    return pl.pallas_call(
        functools.partial(_dsa_prompt_kernel, tq=tq, tkb=tkb, n_sel=n_sel), grid=(b, nq),
        in_specs=[qrow(512), qrow(256), qrow(128), seq(DI_B), seq(128), seq(128)],
        out_specs=qrow(512), out_shape=jax.ShapeDtypeStruct((b * t, 512), BF16),
        scratch_shapes=[pltpu.VMEM((tq, t), jnp.int32), pltpu.VMEM((H_B, tq, 1), F32),
                        pltpu.VMEM((H_B, tq, 1), F32), pltpu.VMEM((H_B, tq, HD_B), F32)], name='dsa_prompt',
        compiler_params=pltpu.CompilerParams(dimension_semantics=('arbitrary', 'arbitrary'),
                                             vmem_limit_bytes=VMEM_LIMIT),
    )(bq, iq, iw, ikb, bkb, bvb)


def _gla_kernel(q_ref, k_ref, v_ref, la_ref, s0_ref, o_ref, sout_ref, st_ref, *, c, nsub):
    j = pl.program_id(1)

    @pl.when(j == 0)
    def _():
        st_ref[...] = s0_ref[0]

    r = lax.broadcasted_iota(jnp.int32, (c, c), 0)
    cc = lax.broadcasted_iota(jnp.int32, (c, c), 1)
    tril = jnp.where(r >= cc, 1.0, 0.0).astype(BF16)
    srow = lax.broadcasted_iota(jnp.int32, (c, DK_C), 0)
    orow = lax.broadcasted_iota(jnp.int32, (c, DV_C), 0)

    def body(i, carry):
        off = pl.multiple_of(i * c, c)
        la = la_ref[0, pl.ds(off, c), :]
        l1, l2, l3 = _split3(la)
        lb = _dot(tril, l1) + _dot(tril, l2) + _dot(tril, l3)
        q = q_ref[0, pl.ds(off, c), :]
        k = k_ref[0, pl.ds(off, c), :]
        v = v_ref[0, pl.ds(off, c), :]
        for h in range(H_C):
            lbh = lb[:, h * DK_C:(h + 1) * DK_C]
            qh = q[:, h * DK_C:(h + 1) * DK_C]
            kh = k[:, h * DK_C:(h + 1) * DK_C]
            vh = v[:, h * DV_C:(h + 1) * DV_C]
            st = st_ref[h]
            o_inter = _dot_nt((qh * jnp.exp(lbh)).astype(BF16), st.astype(BF16))
            o_blk = o_inter
            for t in range(c):
                dec = jnp.exp(jnp.where(srow <= t, lbh[t:t + 1, :] - lbh, -jnp.inf))
                att = jnp.sum(qh[t:t + 1, :] * kh * dec, axis=-1, keepdims=True)
                o_t = jnp.sum(att * vh, axis=0, keepdims=True)
                o_blk = o_blk + jnp.where(orow == t, o_t, 0.0)
            o_ref[0, pl.ds(off, c), h * DV_C:(h + 1) * DV_C] = o_blk
            lb_end = lbh[c - 1:c, :]
            kt = kh * jnp.exp(lb_end - lbh)
            st_ref[h] = st * jnp.exp(lb_end) + _dot_tn(vh.astype(BF16), kt.astype(BF16))
        return carry
    lax.fori_loop(0, nsub, body, 0)

    @pl.when(j == pl.num_programs(1) - 1)
    def _():
        sout_ref[0] = st_ref[...]


def _gla(cq, ck, cv, la, s0t, b, t):
    c = min(GLA_CHUNK, t)
    tt = min(256, t)
    r3 = lambda a: a.reshape(b, t, a.shape[-1])
    tok = lambda w: pl.BlockSpec((1, tt, w), lambda bb, j: (bb, j, 0))
    st = pl.BlockSpec((1, H_C, DV_C, DK_C), lambda bb, j: (bb, 0, 0, 0))
    oc, s_out = pl.pallas_call(
        functools.partial(_gla_kernel, c=c, nsub=tt // c), grid=(b, t // tt),
        in_specs=[tok(256), tok(256), tok(512), tok(256), st],
        out_specs=[tok(512), st],
        out_shape=[jax.ShapeDtypeStruct((b, t, 512), F32), jax.ShapeDtypeStruct((b, H_C, DV_C, DK_C), F32)],
        scratch_shapes=[pltpu.VMEM((H_C, DV_C, DK_C), F32)], name='gla',
        compiler_params=pltpu.CompilerParams(dimension_semantics=('arbitrary', 'arbitrary'),
                                             vmem_limit_bytes=VMEM_LIMIT),
    )(r3(cq), r3(ck), r3(cv), r3(la), s0t)
    return oc.reshape(b * t, 512), s_out


def _back_merge_kernel(x_ref, oa_ref, ob_ref, oc_ref, cr_ref, gate_ref, goc_ref,
                       wpa_ref, wpb_ref, wpc_ref, wo_ref, y_ref):
    oc, cr = oc_ref[...], cr_ref[...]
    parts = []
    for h in range(H_C):
        och = oc[:, h * DV_C:(h + 1) * DV_C]
        parts.append(och * lax.rsqrt(jnp.mean(och * och, axis=-1, keepdims=True) + EPS))
    ocn = jnp.concatenate(parts, axis=1) * goc_ref[...] * (cr * _sigmoid(cr))
    gate = gate_ref[...]
    merged = (_sigmoid(gate[:, :D_MODEL]) * _dot(oa_ref[...], wpa_ref[...])
              + _sigmoid(gate[:, D_MODEL:2 * D_MODEL]) * _dot(ob_ref[...], wpb_ref[...])
              + _sigmoid(gate[:, 2 * D_MODEL:]) * _dot(ocn.astype(BF16), wpc_ref[...]))
    y_ref[...] = x_ref[...] + _dot(merged.astype(BF16), wo_ref[...])


def _back_merge(x2d, oa, ob, oc, cr, gates, lw, tm):
    n = x2d.shape[0]
    row = lambda w: pl.BlockSpec((tm, w), lambda i: (i, 0))
    full = lambda a: pl.BlockSpec(a.shape, lambda i: (0,) * a.ndim)
    consts = [lw['goc'], lw['w_pa'], lw['w_pb'], lw['w_pc'], lw['w_o']]
    return pl.pallas_call(
        _back_merge_kernel, grid=(n // tm,),
        in_specs=[row(D_MODEL), row(512), row(512), row(512), row(512), row(3 * D_MODEL)]
        + [full(a) for a in consts],
        out_specs=row(D_MODEL), out_shape=jax.ShapeDtypeStruct((n, D_MODEL), F32), name='back_merge',
        compiler_params=pltpu.CompilerParams(dimension_semantics=('arbitrary',), vmem_limit_bytes=VMEM_LIMIT),
    )(x2d, oa, ob, oc, cr, gates, *consts)


def _back_ffn_kernel(x_ref, p_ref, gf_ref, wup_ref, wdn_ref, gp_ref, wpg_ref, wpe_ref, y_ref, xn_ref, acc_ref):
    f = pl.program_id(1)

    @pl.when(f == 0)
    def _():
        xn_ref[...] = _rms(x_ref[...], gf_ref[...]).astype(BF16)
        acc_ref[...] = x_ref[...]

    hid = jnp.maximum(_dot(xn_ref[...], wup_ref[...]), 0.0)
    acc_ref[...] += _dot((hid * hid).astype(BF16), wdn_ref[...])

    @pl.when(f == pl.num_programs(1) - 1)
    def _():
        x2 = acc_ref[...]
        gate = _sigmoid(_dot(_rms(x2, gp_ref[...]).astype(BF16), wpg_ref[...]))
        y_ref[...] = x2 + gate * _dot(p_ref[...].astype(BF16), wpe_ref[...])


def _back_ffn(x2d, p2d, lw, tm):
    n = x2d.shape[0]
    tf = 1024
    row = lambda w: pl.BlockSpec((tm, w), lambda i, f: (i, 0))
    full = lambda a: pl.BlockSpec(a.shape, lambda i, f: (0,) * a.ndim)
    return pl.pallas_call(
        _back_ffn_kernel, grid=(n // tm, D_FF // tf),
        in_specs=[row(D_MODEL), row(D_PLE), full(lw['g_ffn']),
                  pl.BlockSpec((D_MODEL, tf), lambda i, f: (0, f)), pl.BlockSpec((tf, D_MODEL), lambda i, f: (f, 0)),
                  full(lw['g_ple']), full(lw['w_pg']), full(lw['w_pe'])],
        out_specs=row(D_MODEL), out_shape=jax.ShapeDtypeStruct((n, D_MODEL), F32),
        scratch_shapes=[pltpu.VMEM((tm, D_MODEL), BF16), pltpu.VMEM((tm, D_MODEL), F32)], name='back_ffn',
        compiler_params=pltpu.CompilerParams(dimension_semantics=('arbitrary', 'arbitrary'),
                                             vmem_limit_bytes=VMEM_LIMIT),
    )(x2d, p2d, lw['g_ffn'], lw['w_up'], lw['w_down'], lw['g_ple'], lw['w_pg'], lw['w_pe'])


def _mla_decode_kernel(pt_ref, q_ref, iq_ref, iw_ref, cn_ref, krn_ref, kinvn_ref, ikn_ref,
                       wuk_ref, e8_ref, wuv_ref, *rest, npg, t):
    ckv = rest[:npg]
    krp = rest[npg:2 * npg]
    kix = rest[2 * npg:3 * npg]
    o_ref, keys_ref, keysn_ref, m_ref, l_ref, acc_ref = rest[3 * npg:]
    g = pl.program_id(1)
    rows = H_A * t

    @pl.when(g == 0)
    def _():
        m_ref[...] = jnp.full(m_ref.shape, NEG, F32)
        l_ref[...] = jnp.zeros(l_ref.shape, F32)
        acc_ref[...] = jnp.zeros(acc_ref.shape, F32)

    q = q_ref[0]
    qlat, qr = q[:, :R_KV], q[:, R_KV:R_KV + DR_A]
    iq = iq_ref[0]
    iw = iw_ref[0]

    def attend(cb16, krb16, kinv, mask):
        s = _dot_nt(qlat, cb16) + _dot_nt(qr, krb16)
        s = s * jnp.concatenate([jnp.broadcast_to(kinv[h:h + 1, :], (t, kinv.shape[1])) for h in range(H_A)], 0)
        if mask is not None:
            s = jnp.where(mask, s, NEG)
        m_prev = m_ref[...]
        m_new = jnp.maximum(m_prev, jnp.max(s, axis=-1, keepdims=True))
        p = jnp.exp(s - m_new)
        alpha = jnp.exp(m_prev - m_new)
        l_ref[...] = alpha * l_ref[...] + jnp.sum(p, axis=-1, keepdims=True)
        acc_ref[...] = alpha * acc_ref[...] + _dot(p.astype(BF16), cb16)
        m_ref[...] = m_new

    def index_scores(ikb16):
        d = jnp.maximum(_dot_nt(iq, ikb16), 0.0)
        sc = jnp.zeros((t, d.shape[1]), F32)
        for h in range(HI_B):
            sc = sc + d[h * t:(h + 1) * t, :] * iw[h * t:(h + 1) * t, :1]
        return sc

    e8 = e8_ref[...]
    ones8 = jnp.ones((8, DR_A), BF16)
    for pg in range(npg):
        c = ckv[pg][...]
        kr = krp[pg][...]
        cb16 = c.astype(BF16)
        kn = _dot(cb16, wuk_ref[...])
        ms = _dot_nt_f32(e8, kn * kn) + _dot_nt_f32(ones8, kr * kr)
        attend(cb16, kr.astype(BF16), lax.rsqrt(ms * (1.0 / (DN_A + DR_A)) + EPS), None)
        keys_ref[0, :, pg * PAGE:(pg + 1) * PAGE] = _score_key(index_scores(kix[pg][...].astype(BF16)))

    @pl.when(g == pl.num_programs(1) - 1)
    def _():
        pad = lambda a: jnp.concatenate([a, jnp.zeros((PAGE - t, a.shape[1]), a.dtype)], axis=0)
        col = lax.broadcasted_iota(jnp.int32, (rows, PAGE), 1)
        qt = lax.broadcasted_iota(jnp.int32, (rows, PAGE), 0) % t
        attend(pad(cn_ref[0]).astype(BF16), pad(krn_ref[0]).astype(BF16), kinvn_ref[0], col <= qt)
        scn = jnp.where((col <= qt)[:t], index_scores(pad(ikn_ref[0]).astype(BF16)), -jnp.inf)
        keysn_ref[0] = _score_key(scn)
        o = acc_ref[...] / l_ref[...]
        olat = jnp.concatenate([o[h * t:(h + 1) * t, :] for h in range(H_A)], axis=1).astype(BF16)
        o_ref[0] = _dot(olat, wuv_ref[...]).astype(BF16)


def _mla_decode(pt_flat, layer, qs, iqs, iws, cn, krn, kinvn, ikn, lw, cache_ckv, cache_krope, cache_kidx,
                b, t, n_pages):
    npg = min(16, n_pages)
    ng = n_pages // npg
    past = n_pages * PAGE
    per_b = lambda a: pl.BlockSpec((1,) + a.shape[1:], lambda bb, g, pt: (bb,) + (0,) * (a.ndim - 1))
    full = lambda a: pl.BlockSpec(a.shape, lambda bb, g, pt: (0,) * a.ndim)

    def page(w, pg):
        return pl.BlockSpec((None, None, PAGE, w),
                            lambda bb, g, pt: (layer, pt[bb * n_pages + g * npg + pg], 0, 0))
    small = [qs, iqs, iws, cn, krn, kinvn, ikn]
    consts = [lw['wuk'], lw['e8'], lw['wuv_bd']]
    in_specs = ([per_b(a) for a in small] + [full(a) for a in consts]
                + [page(R_KV, pg) for pg in range(npg)] + [page(DR_A, pg) for pg in range(npg)]
                + [page(DI_B, pg) for pg in range(npg)])
    grid_spec = pltpu.PrefetchScalarGridSpec(
        num_scalar_prefetch=1, grid=(b, ng), in_specs=in_specs,
        out_specs=[pl.BlockSpec((1, t, 512), lambda bb, g, pt: (bb, 0, 0)),
                   pl.BlockSpec((1, t, npg * PAGE), lambda bb, g, pt: (bb, 0, g)),
                   pl.BlockSpec((1, t, PAGE), lambda bb, g, pt: (bb, 0, 0))],
        scratch_shapes=[pltpu.VMEM((H_A * t, 1), F32), pltpu.VMEM((H_A * t, 1), F32),
                        pltpu.VMEM((H_A * t, R_KV), F32)])
    return pl.pallas_call(
        functools.partial(_mla_decode_kernel, npg=npg, t=t), grid_spec=grid_spec,
        out_shape=[jax.ShapeDtypeStruct((b, t, 512), BF16), jax.ShapeDtypeStruct((b, t, past), jnp.int32),
                   jax.ShapeDtypeStruct((b, t, PAGE), jnp.int32)], name='mla_decode',
        compiler_params=pltpu.CompilerParams(dimension_semantics=('arbitrary', 'arbitrary'),
                                             vmem_limit_bytes=VMEM_LIMIT),
    )(pt_flat, *small, *consts, *([cache_ckv] * npg), *([cache_krope] * npg), *([cache_kidx] * npg))


def _dsa_decode_kernel(pt_ref, keys_ref, keysn_ref, bq_ref, kn_ref, vn_ref, *rest, npg, t, n_sel):
    kp = rest[:npg]
    vp = rest[npg:2 * npg]
    o_ref, kth_ref, need_ref, carry_ref, m_ref, l_ref, acc_ref = rest[2 * npg:]
    g = pl.program_id(1)
    nk = npg * PAGE
    gsz = H_B // KVH_B

    @pl.when(g == 0)
    def _():
        keys, keysn = keys_ref[0], keysn_ref[0]

        def count(pred):
            return (jnp.sum(jnp.where(pred(keys), 1.0, 0.0), axis=-1, keepdims=True)
                    + jnp.sum(jnp.where(pred(keysn), 1.0, 0.0), axis=-1, keepdims=True))
        kth = _kth_largest(lambda cand: count(lambda blk: blk >= cand), float(n_sel), t)
        kth_ref[...] = kth
        need_ref[...] = float(n_sel) - count(lambda blk: blk > kth)
        carry_ref[...] = jnp.zeros(carry_ref.shape, F32)
        m_ref[...] = jnp.full(m_ref.shape, NEG, F32)
        l_ref[...] = jnp.zeros(l_ref.shape, F32)
        acc_ref[...] = jnp.zeros(acc_ref.shape, F32)

    q = bq_ref[0]

    def attend(keys_blk, k16, v16, extra):
        sel, carry = _select(keys_blk, kth_ref[...], need_ref[...], carry_ref[...])
        carry_ref[...] = carry
        if extra is not None:
            sel = sel & extra
        selg = jnp.concatenate([jnp.where(sel, 1.0, 0.0)] * gsz, axis=0) > 0.0
        for n in range(KVH_B):
            rs = slice(n * gsz * t, (n + 1) * gsz * t)
            s = _dot_nt(q[rs, :], k16[:, n * HD_B:(n + 1) * HD_B])
            s = jnp.where(selg, s, NEG)
            m_prev = m_ref[rs, :]
            m_new = jnp.maximum(m_prev, jnp.max(s, axis=-1, keepdims=True))
            p = jnp.where(selg, jnp.exp(s - m_new), 0.0)
            alpha = jnp.exp(m_prev - m_new)
            l_ref[rs, :] = alpha * l_ref[rs, :] + jnp.sum(p, axis=-1, keepdims=True)
            acc_ref[rs, :] = alpha * acc_ref[rs, :] + _dot(p.astype(BF16), v16[:, n * HD_B:(n + 1) * HD_B])
            m_ref[rs, :] = m_new

    k16 = jnp.concatenate([kp[pg][...].astype(BF16) for pg in range(npg)], axis=0)
    v16 = jnp.concatenate([vp[pg][...].astype(BF16) for pg in range(npg)], axis=0)
    off = pl.multiple_of(g * nk, nk)
    attend(keys_ref[0, :, pl.ds(off, nk)], k16, v16, None)

    @pl.when(g == pl.num_programs(1) - 1)
    def _():
        pad = lambda a: jnp.concatenate([a, jnp.zeros((PAGE - t, a.shape[1]), a.dtype)], axis=0)
        col = lax.broadcasted_iota(jnp.int32, (t, PAGE), 1)
        qt = lax.broadcasted_iota(jnp.int32, (t, PAGE), 0)
        attend(keysn_ref[0], pad(kn_ref[0]).astype(BF16), pad(vn_ref[0]).astype(BF16), col <= qt)
        o = acc_ref[...] / l_ref[...]
        o_ref[0] = jnp.concatenate([o[h * t:(h + 1) * t, :] for h in range(H_B)], axis=1).astype(BF16)


def _dsa_decode(pt_flat, layer, keys, keysn, bqs, kn, vn, cache_k, cache_v, b, t, n_pages):
    npg = min(16, n_pages)
    ng = n_pages // npg
    past = n_pages * PAGE
    n_sel = min(TOPK_MAX, (past + t) // 4)
    per_b = lambda a: pl.BlockSpec((1,) + a.shape[1:], lambda bb, g, pt: (bb,) + (0,) * (a.ndim - 1))

    def page(pg):
        return pl.BlockSpec((None, None, PAGE, KVH_B * HD_B),
                            lambda bb, g, pt: (layer, pt[bb * n_pages + g * npg + pg], 0, 0))
    small = [keys, keysn, bqs, kn, vn]
    grid_spec = pltpu.PrefetchScalarGridSpec(
        num_scalar_prefetch=1, grid=(b, ng),
        in_specs=[per_b(a) for a in small] + [page(pg) for pg in range(npg)] * 2,
        out_specs=pl.BlockSpec((1, t, 512), lambda bb, g, pt: (bb, 0, 0)),
        scratch_shapes=[pltpu.VMEM((t, 1), jnp.int32), pltpu.VMEM((t, 1), F32), pltpu.VMEM((t, 1), F32),
                        pltpu.VMEM((H_B * t, 1), F32), pltpu.VMEM((H_B * t, 1), F32),
                        pltpu.VMEM((H_B * t, HD_B), F32)])
    ck = cache_k.reshape(cache_k.shape[:3] + (KVH_B * HD_B,))
    cv = cache_v.reshape(cache_v.shape[:3] + (KVH_B * HD_B,))
    return pl.pallas_call(
        functools.partial(_dsa_decode_kernel, npg=npg, t=t, n_sel=n_sel), grid_spec=grid_spec,
        out_shape=jax.ShapeDtypeStruct((b, t, 512), BF16), name='dsa_decode',
        compiler_params=pltpu.CompilerParams(dimension_semantics=('arbitrary', 'arbitrary'),
                                             vmem_limit_bytes=VMEM_LIMIT),
    )(pt_flat, *small, *([ck] * npg), *([cv] * npg))


def _indicator(width, seg):
    e = np.zeros((width, LANE), np.float32)
    e[np.arange(width), np.arange(width) // seg] = 1.0
    return e


def _prep_layer(w, l):
    pad = lambda a, n: jnp.pad(a, ((0, 0), (0, n - a.shape[1])))
    w_in = w['w_in'][l]
    o_b, o_c, o_g = N_A, N_A + N_B, N_A + N_B + N_C
    w_uq = w['w_uq'][l].reshape(R_Q, H_A, DN_A + DR_A)
    wq_r = jnp.pad(w_uq[:, :, DN_A:], ((0, 0), (0, 0), (0, LANE - DR_A))).reshape(R_Q, H_A * LANE)
    g_q_a = w['g_q_a'][l]
    w_uk = w['w_uk'][l]
    eye = jnp.eye(H_A, dtype=F32)
    wabs = (eye[:, None, :, None] * jnp.transpose(w_uk, (1, 2, 0))[:, :, None, :]).reshape(H_A * DN_A, H_A * R_KV)
    wuv_bd = (eye[:, None, :, None] * jnp.transpose(w['w_uv'][l], (1, 0, 2))[:, :, None, :]
              ).reshape(H_A * R_KV, H_A * DV_A)
    e64 = _indicator(512, 64)
    e128 = _indicator(1024, 128)
    row = lambda a: a.reshape(1, -1)
    return dict(
        g_attn=row(w['g_attn'][l]),
        wa=pad(w_in[:, :o_b], W_A).astype(BF16), wb=pad(w_in[:, o_b:o_c], W_B).astype(BF16),
        wc=pad(w_in[:, o_c:o_g], W_C).astype(BF16), wg=w_in[:, o_g:].astype(BF16),
        g_qa=row(w['g_qa'][l]),
        wq2=jnp.concatenate([w_uq[:, :, :DN_A].reshape(R_Q, H_A * DN_A), wq_r], axis=1).astype(BF16),
        gn=row(jnp.tile(g_q_a[:DN_A], H_A)),
        gr=row(jnp.tile(jnp.pad(g_q_a[DN_A:], (0, LANE - DR_A)), H_A)),
        g_kva=row(w['g_kva'][l]),
        wuk=w_uk.reshape(R_KV, H_A * DN_A).astype(BF16),
        wabs=(wabs * SCALE_A).astype(BF16), wuv_bd=wuv_bd.astype(BF16),
        gqb=row(jnp.tile(w['g_q_b'][l], H_B)), gkb=row(jnp.tile(w['g_k_b'][l], KVH_B)),
        e64=jnp.asarray(e64, BF16), e64t=jnp.asarray(e64.T, BF16),
        e128=jnp.asarray(e128, BF16), e128t=jnp.asarray(e128.T, BF16),
        e8=jnp.asarray(e64.T[:8], BF16),
        wa2=jnp.pad(w['w_a2'][l], ((0, LANE - R_GATE_C), (0, 0))).astype(BF16), b_a=row(w['b_a'][l]),
        goc=row(jnp.tile(w['g_o_c'][l], H_C)),
        w_pa=w['w_pa'][l].astype(BF16), w_pb=w['w_pb'][l].astype(BF16), w_pc=w['w_pc'][l].astype(BF16),
        w_o=w['w_o'][l].astype(BF16), g_ffn=row(w['g_ffn'][l]),
        w_up=w['w_up'][l].astype(BF16), w_down=w['w_down'][l].astype(BF16),
        g_ple=row(w['g_ple'][l]), w_pe=w['w_pe'][l].astype(BF16), w_pg=w['w_pg'][l].astype(BF16),
    )


def _rope_tables(pos):
    def tab(rot, period):
        inv = ROPE_THETA ** (-jnp.arange(0, rot, 2, dtype=F32) / rot)
        ang = pos.astype(F32)[:, None] * inv[None, :]
        c, s = jnp.cos(ang), jnp.sin(ang)
        npass = period - rot
        cos = jnp.concatenate([c, c, jnp.ones((pos.shape[0], npass), F32)], axis=1)
        sin = jnp.concatenate([-s, s, jnp.zeros((pos.shape[0], npass), F32)], axis=1)
        return jnp.tile(cos, (1, LANE // period)), jnp.tile(sin, (1, LANE // period))
    ca, sa = tab(DR_A, LANE)
    cb, sb = tab(ROT_B, HD_B)
    return ca, sa, cb, sb


def _head_major(a, b, t, nh, d):
    return a.reshape(b, t, nh, d).transpose(0, 2, 1, 3).reshape(b, nh * t, d)


def _layer(xp, xs, pp, ps, lw, l, caches, s_prev, pt_flat, tabs_p, tabs_s, dims):
    bp, tp, bs, ts, n_pages = dims
    cache_ckv, cache_krope, cache_k, cache_v, cache_kidx = caches
    np_, ns = bp * tp, bs * ts
    tm_p, tm_s = min(256, np_), min(256, ns)

    (q, kcat, c, kr, kinv, bq, bk, bkb, bv, bvb, iq, ik, ikb, iw) = _front_ab(xp, tabs_p, lw, tm_p)
    cq, ck, cv, la, cr, gates = _front_cg(xp, lw, tm_p)
    oa = _mla_prompt(q, kcat, kinv, lw['wuv_bd'], bp, tp)
    ob = _dsa_prompt(bq, iq, iw, ikb, bkb, bvb, bp, tp)
    oc, st = _gla(cq, ck, cv, la, jnp.zeros((bp, H_C, DV_C, DK_C), F32), bp, tp)
    x1 = _back_merge(xp, oa, ob, oc, cr, gates, lw, tm_p)
    yp = _back_ffn(x1, pp, lw, min(512, np_))
    rows_p = (c.reshape(bp, tp, R_KV), kr.reshape(bp, tp, DR_A), bk.reshape(bp, tp, KVH_B, HD_B),
              bv.reshape(bp, tp, KVH_B, HD_B), ik.reshape(bp, tp, DI_B), jnp.swapaxes(st, 2, 3))

    (q, kcat, c, kr, kinv, bq, bk, bkb, bv, bvb, iq, ik, ikb, iw) = _front_ab(xs, tabs_s, lw, tm_s)
    cq, ck, cv, la, cr, gates = _front_cg(xs, lw, tm_s)
    qs = _head_major(q, bs, ts, H_A, 256)
    iqs = _head_major(iq, bs, ts, HI_B, DI_B)
    iws = jnp.broadcast_to(iw[:, DI_B:DI_B + HI_B].reshape(bs, ts, HI_B).transpose(0, 2, 1).reshape(bs, HI_B * ts, 1),
                           (bs, HI_B * ts, LANE))
    kinvn = jnp.pad(kinv.reshape(8, bs, ts).transpose(1, 0, 2), ((0, 0), (0, 0), (0, PAGE - ts)))
    oa, keys, keysn = _mla_decode(pt_flat, l, qs, iqs, iws, c.reshape(bs, ts, R_KV), kr.reshape(bs, ts, DR_A),
                                  kinvn, ik.reshape(bs, ts, DI_B), lw, cache_ckv, cache_krope, cache_kidx,
                                  bs, ts, n_pages)
    ob = _dsa_decode(pt_flat, l, keys, keysn, _head_major(bq, bs, ts, H_B, HD_B),
                     bk.reshape(bs, ts, 128), bv.reshape(bs, ts, 128), cache_k, cache_v, bs, ts, n_pages)
    oc, st = _gla(cq, ck, cv, la, jnp.swapaxes(s_prev, 2, 3), bs, ts)
    x1 = _back_merge(xs, oa.reshape(ns, 512), ob.reshape(ns, 512), oc, cr, gates, lw, tm_s)
    ys = _back_ffn(x1, ps, lw, min(512, ns))
    rows_s = (c.reshape(bs, ts, R_KV), kr.reshape(bs, ts, DR_A), bk.reshape(bs, ts, KVH_B, HD_B),
              bv.reshape(bs, ts, KVH_B, HD_B), ik.reshape(bs, ts, DI_B), jnp.swapaxes(st, 2, 3))
    return yp, ys, rows_p, rows_s


def kernel(x_prompt, x_sample, cache_ckv, cache_krope, cache_k, cache_v, cache_kidx, state_gla, page_table,
           p_prompt, p_sample, g_attn, w_in, g_qa, w_uq, g_q_a, g_kva, w_uk, w_uv, g_q_b, g_k_b, w_a2, b_a,
           g_o_c, w_pa, w_pb, w_pc, w_o, g_ffn, w_up, w_down, g_ple, w_pe, w_pg):
    w = dict(g_attn=g_attn, w_in=w_in, g_qa=g_qa, w_uq=w_uq, g_q_a=g_q_a, g_kva=g_kva, w_uk=w_uk, w_uv=w_uv,
             g_q_b=g_q_b, g_k_b=g_k_b, w_a2=w_a2, b_a=b_a, g_o_c=g_o_c, w_pa=w_pa, w_pb=w_pb, w_pc=w_pc,
             w_o=w_o, g_ffn=g_ffn, w_up=w_up, w_down=w_down, g_ple=g_ple, w_pe=w_pe, w_pg=w_pg)
    depth = w_in.shape[0]
    bp, tp, _ = x_prompt.shape
    bs, ts, _ = x_sample.shape
    n_pages = page_table.shape[1]
    past = n_pages * PAGE
    np_, ns = bp * tp, bs * ts
    tabs_p = _rope_tables(jnp.arange(tp))
    tm_s = min(256, ns)
    tabs_s = _rope_tables(past + (jnp.arange(tm_s) % ts))
    pt_flat = page_table.reshape(-1)
    xp, xs = x_prompt.reshape(np_, D_MODEL), x_sample.reshape(ns, D_MODEL)
    new_p, new_s = [], []
    for l in range(depth):
        lw = _prep_layer(w, l)
        xp, xs, rows_p, rows_s = _layer(
            xp, xs, p_prompt[l].reshape(np_, D_PLE), p_sample[l].reshape(ns, D_PLE), lw, l,
            (cache_ckv, cache_krope, cache_k, cache_v, cache_kidx), state_gla[l], pt_flat, tabs_p, tabs_s,
            (bp, tp, bs, ts, n_pages))
        new_p.append(rows_p)
        new_s.append(rows_s)
    stack = lambda rows, i: jnp.stack([r[i] for r in rows], axis=0)
    return ((xp.reshape(bp, tp, D_MODEL), xs.reshape(bs, ts, D_MODEL))
            + tuple(stack(new_p, i) for i in range(6)) + tuple(stack(new_s, i) for i in range(6)))
```

```python
import functools

import numpy as np
import jax
import jax.numpy as jnp
from jax import lax
from jax.experimental import pallas as pl
from jax.experimental.pallas import tpu as pltpu

D_MODEL = 1024
PAGE = 128
H_A, DN_A, DR_A, DV_A, R_Q, R_KV = 8, 64, 32, 64, 256, 128
SCALE_A = (DN_A + DR_A) ** -0.5
H_B, KVH_B, HD_B = 8, 2, 64
ROT_B = HD_B // 4
HI_B, DI_B = 4, 64
TOPK_MAX = 256
H_C, DK_C, DV_C, R_GATE_C = 4, 64, 128, 16
GLA_TAU = 16.0
GLA_BLOCK = 64
QHEAD_ORDER = (0, 4, 1, 5, 2, 6, 3, 7)
D_FF = 4 * D_MODEL
D_PLE = 256
ROPE_THETA = 500000.0
EPS = 1e-6

N_A = R_Q + R_KV + DR_A
N_B = H_B * HD_B + 2 * KVH_B * HD_B + HI_B * DI_B + DI_B + HI_B
N_C = 2 * H_C * DK_C + 2 * H_C * DV_C + R_GATE_C
W_A, W_B, W_C = 512, 1152, 1664
LANE = 128
NEG = -1e30
INT_MIN = -(2 ** 31)

F32 = jnp.float32
BF16 = jnp.bfloat16
VMEM_LIMIT = 48 * 1024 * 1024


def _dot(a, b):
    return jnp.dot(a, b, preferred_element_type=F32)


def _dot_nt(a, b):
    return lax.dot_general(a, b, (((1,), (1,)), ((), ())), preferred_element_type=F32)


def _dot_tn(a, b):
    return lax.dot_general(a, b, (((0,), (0,)), ((), ())), preferred_element_type=F32)


def _split3(a):
    a1 = a.astype(BF16)
    r1 = a - a1.astype(F32)
    a2 = r1.astype(BF16)
    a3 = (r1 - a2.astype(F32)).astype(BF16)
    return a1, a2, a3


def _dot_f32(a, b_bf):
    a1, a2, a3 = _split3(a)
    return _dot(a1, b_bf) + _dot(a2, b_bf) + _dot(a3, b_bf)


def _dot_nt_f32(a_bf, b):
    b1, b2, b3 = _split3(b)
    return _dot_nt(a_bf, b1) + _dot_nt(a_bf, b2) + _dot_nt(a_bf, b3)


def _rms(x, g):
    return x * lax.rsqrt(jnp.mean(x * x, axis=-1, keepdims=True) + EPS) * g


def _seg_inv(x, e, et, n):
    inv = lax.rsqrt(_dot_f32(x * x, e) * (1.0 / n) + EPS)
    return _dot_f32(inv, et)


def _rope(x, cos, sin, period, half):
    w = x.shape[-1]
    reps = w // LANE
    if reps > 1:
        cos = jnp.concatenate([cos] * reps, axis=1)
        sin = jnp.concatenate([sin] * reps, axis=1)
    lane = lax.broadcasted_iota(jnp.int32, x.shape, 1)
    first = (lane % period) < half
    partner = jnp.where(first, pltpu.roll(x, w - half, 1), pltpu.roll(x, half, 1))
    return x * cos + partner * sin


def _sigmoid(x):
    return 1.0 / (1.0 + jnp.exp(-x))


def _front_ab_kernel(x_ref, g_ref, wa_ref, wb_ref, gqa_ref, wq_ref, gn_ref, gr_ref, gkv_ref, wuk_ref,
                     wabs_ref, gqb_ref, gkb_ref, ca_ref, sa_ref, cb_ref, sb_ref,
                     e64_ref, e64t_ref, e128_ref, e128t_ref, e8_ref,
                     q_ref, kcat_ref, c_ref, kr_ref, kinv_ref,
                     bq_ref, bk_ref, bkb_ref, bv_ref, bvt_ref, iq_ref, ik_ref, ikb_ref, iwt_ref):
    xn = _rms(x_ref[...], g_ref[...]).astype(BF16)
    za = _dot(xn, wa_ref[...])
    zb = _dot(xn, wb_ref[...])
    ca, sa, cb, sb = ca_ref[...], sa_ref[...], cb_ref[...], sb_ref[...]
    e64, e64t = e64_ref[...], e64t_ref[...]

    aq = _rms(za[:, :R_Q], gqa_ref[...]).astype(BF16)
    q2 = _dot(aq, wq_ref[...])
    qn, qr = q2[:, :512], q2[:, 512:]
    ss = _dot_f32(qn * qn, e64) + _dot_f32(qr * qr, e128_ref[...])
    inv = lax.rsqrt(ss * (1.0 / (DN_A + DR_A)) + EPS)
    qn = qn * _dot_f32(inv, e64t) * gn_ref[...]
    qr = qr * _dot_f32(inv, e128t_ref[...]) * gr_ref[...]
    qr = _rope(qr, ca, sa, LANE, DR_A // 2) * SCALE_A
    qlat = _dot(qn.astype(BF16), wabs_ref[...])
    for h in range(H_A):
        q_ref[:, h * 256:h * 256 + 128] = qlat[:, h * 128:(h + 1) * 128].astype(BF16)
        q_ref[:, h * 256 + 128:(h + 1) * 256] = qr[:, h * 128:(h + 1) * 128].astype(BF16)

    c = _rms(za[:, R_Q:R_Q + R_KV], gkv_ref[...])
    kr = _rope(za[:, R_Q + R_KV:], ca, sa, LANE, DR_A // 2)
    c_ref[...] = c
    kr_ref[...] = kr[:, :DR_A]
    cb16 = c.astype(BF16)
    kcat_ref[:, :R_KV] = cb16
    kcat_ref[:, R_KV:] = kr.astype(BF16)
    kn = _dot(cb16, wuk_ref[...])
    e8 = e8_ref[...]
    ones8 = jnp.ones((8, LANE), BF16)
    ms = _dot_nt_f32(e8, kn * kn) + _dot_nt_f32(ones8, kr * kr)
    kinv_ref[...] = lax.rsqrt(ms * (1.0 / (DN_A + DR_A)) + EPS)

    bq = zb[:, :512]
    bq = bq * _seg_inv(bq, e64, e64t, HD_B) * gqb_ref[...]
    bq_ref[...] = (_rope(bq, cb, sb, HD_B, ROT_B // 2) * (HD_B ** -0.5)).astype(BF16)
    bk = zb[:, 512:640]
    bk = bk * _seg_inv(bk, e64[:128], e64t[:, :128], HD_B) * gkb_ref[...]
    bk = _rope(bk, cb, sb, HD_B, ROT_B // 2)
    bk_ref[...] = bk
    bkb_ref[...] = bk.astype(BF16)
    bv = zb[:, 640:768]
    bv_ref[...] = bv
    bvt_ref[...] = bv.T.astype(BF16)
    iq_ref[...] = (_rope(zb[:, 768:1024], cb, sb, DI_B, ROT_B // 2) * (DI_B ** -0.5)).astype(BF16)
    last = zb[:, 1024:1152]
    ik = _rope(last, cb, sb, DI_B, ROT_B // 2)[:, :DI_B]
    ik_ref[...] = ik
    ikb_ref[...] = ik.astype(BF16)
    iwt_ref[...] = (last * (HI_B ** -0.5)).T[DI_B:DI_B + 8, :]


def _front_ab(x2d, tabs, lw, tm):
    n = x2d.shape[0]
    nblk = tabs[0].shape[0] // tm
    row = lambda w: pl.BlockSpec((tm, w), lambda i: (i, 0))
    full = lambda a: pl.BlockSpec(a.shape, lambda i: (0,) * a.ndim)
    tab = pl.BlockSpec((tm, LANE), lambda i: (i % nblk, 0))
    consts = [lw['g_attn'], lw['wa'], lw['wb'], lw['g_qa'], lw['wq2'], lw['gn'], lw['gr'], lw['g_kva'],
              lw['wuk'], lw['wabs'], lw['gqb'], lw['gkb']]
    inds = [lw['e64'], lw['e64t'], lw['e128'], lw['e128t'], lw['e8']]
    outs = [(2048, BF16, None), (256, BF16, None), (R_KV, F32, None), (DR_A, F32, None), (8, F32),
            (512, BF16, None), (128, F32, None), (128, BF16, None), (128, F32, None), (128, BF16),
            (256, BF16, None), (DI_B, F32, None), (DI_B, BF16, None), (8, F32)]
    out_shape, out_specs = [], []
    for o in outs:
        if len(o) == 2:
            out_shape.append(jax.ShapeDtypeStruct((o[0], n), o[1]))
            out_specs.append(pl.BlockSpec((o[0], tm), lambda i: (0, i)))
        else:
            out_shape.append(jax.ShapeDtypeStruct((n, o[0]), o[1]))
            out_specs.append(row(o[0]))
    return pl.pallas_call(
        _front_ab_kernel, grid=(n // tm,),
        in_specs=[row(D_MODEL)] + [full(a) for a in consts] + [tab] * 4 + [full(a) for a in inds],
        out_specs=out_specs, out_shape=out_shape, name='front_ab',
        compiler_params=pltpu.CompilerParams(dimension_semantics=('arbitrary',), vmem_limit_bytes=VMEM_LIMIT),
    )(x2d, *consts, *tabs, *inds)


def _front_cg_kernel(x_ref, g_ref, wc_ref, wg_ref, wa2_ref, ba_ref,
                     cq_ref, ck_ref, cv_ref, la_ref, cr_ref, gate_ref):
    xn = _rms(x_ref[...], g_ref[...]).astype(BF16)
    zc = _dot(xn, wc_ref[...])
    cq_ref[...] = zc[:, :256]
    ck_ref[...] = zc[:, 256:512] * (DK_C ** -0.5)
    cv_ref[...] = zc[:, 512:1024]
    cr_ref[...] = zc[:, 1024:1536]
    u = _dot(zc[:, 1536:1664].astype(BF16), wa2_ref[...]) + ba_ref[...]
    la_ref[...] = (jnp.minimum(u, 0.0) - jnp.log1p(jnp.exp(-jnp.abs(u)))) * (1.0 / GLA_TAU)
    gate_ref[...] = _dot(xn, wg_ref[...])


def _front_cg(x2d, lw, tm):
    n = x2d.shape[0]
    row = lambda w: pl.BlockSpec((tm, w), lambda i: (i, 0))
    full = lambda a: pl.BlockSpec(a.shape, lambda i: (0,) * a.ndim)
    consts = [lw['g_attn'], lw['wc'], lw['wg'], lw['wa2'], lw['b_a']]
    widths = [256, 256, 512, 256, 512, 3 * D_MODEL]
    return pl.pallas_call(
        _front_cg_kernel, grid=(n // tm,),
        in_specs=[row(D_MODEL)] + [full(a) for a in consts],
        out_specs=[row(w) for w in widths],
        out_shape=[jax.ShapeDtypeStruct((n, w), F32) for w in widths], name='front_cg',
        compiler_params=pltpu.CompilerParams(dimension_semantics=('arbitrary',), vmem_limit_bytes=VMEM_LIMIT),
    )(x2d, *consts)


def _mla_prompt_kernel(q_ref, k_ref, kinv_ref, wuv_ref, o_ref, m_ref, l_ref, acc_ref, *, tq, tk):
    i, j = pl.program_id(1), pl.program_id(2)

    @pl.when(j == 0)
    def _():
        m_ref[...] = jnp.full(m_ref.shape, NEG, F32)
        l_ref[...] = jnp.zeros(l_ref.shape, F32)
        acc_ref[...] = jnp.zeros(acc_ref.shape, F32)

    @pl.when(j <= i)
    def _():
        k = k_ref[...]
        v = k[:, :R_KV]
        kinv = kinv_ref[...]
        rowp = i * tq + lax.broadcasted_iota(jnp.int32, (tq, tk), 0)
        colp = j * tk + lax.broadcasted_iota(jnp.int32, (tq, tk), 1)
        mask = colp <= rowp
        for h in range(H_A):
            s = _dot_nt(q_ref[:, h * 256:(h + 1) * 256], k) * kinv[h:h + 1, :]
            s = jnp.where(mask, s, NEG)
            m_prev = m_ref[h]
            m_new = jnp.maximum(m_prev, jnp.max(s, axis=-1, keepdims=True))
            p = jnp.exp(s - m_new)
            alpha = jnp.exp(m_prev - m_new)
            l_ref[h] = alpha * l_ref[h] + jnp.sum(p, axis=-1, keepdims=True)
            acc_ref[h] = alpha * acc_ref[h] + _dot(p.astype(BF16), v)
            m_ref[h] = m_new

    @pl.when(j == pl.num_programs(2) - 1)
    def _():
        olat = jnp.concatenate([(acc_ref[h] / l_ref[h]).astype(BF16) for h in range(H_A)], axis=1)
        o_ref[...] = _dot(olat, wuv_ref[...]).astype(BF16)


def _mla_prompt(q, kcat, kinv, wuv_bd, b, t):
    tq = tk = min(256, t)
    nq = t // tq
    return pl.pallas_call(
        functools.partial(_mla_prompt_kernel, tq=tq, tk=tk), grid=(b, nq, nq),
        in_specs=[pl.BlockSpec((tq, 2048), lambda bb, i, j: (bb * nq + i, 0)),
                  pl.BlockSpec((tk, 256), lambda bb, i, j: (bb * nq + jnp.minimum(j, i), 0)),
                  pl.BlockSpec((8, tk), lambda bb, i, j: (0, bb * nq + jnp.minimum(j, i))),
                  pl.BlockSpec(wuv_bd.shape, lambda bb, i, j: (0, 0))],
        out_specs=pl.BlockSpec((tq, 512), lambda bb, i, j: (bb * nq + i, 0)),
        out_shape=jax.ShapeDtypeStruct((b * t, 512), BF16),
        scratch_shapes=[pltpu.VMEM((H_A, tq, 1), F32), pltpu.VMEM((H_A, tq, 1), F32),
                        pltpu.VMEM((H_A, tq, R_KV), F32)], name='mla_prompt',
        compiler_params=pltpu.CompilerParams(dimension_semantics=('arbitrary', 'arbitrary', 'arbitrary'),
                                             vmem_limit_bytes=VMEM_LIMIT),
    )(q, kcat, kinv, wuv_bd)


def _score_key(sc):
    sc = jnp.where(sc == 0.0, 0.0, sc)
    bits = lax.bitcast_convert_type(sc, jnp.int32)
    return jnp.where(bits < 0, bits ^ 0x7FFFFFFF, bits)


def _kth_largest(count_ge, k, shape):
    base = jnp.where(count_ge(jnp.zeros(shape, jnp.int32)) >= k, 0, INT_MIN).astype(jnp.int32)

    def body(it, base):
        cand = base + jnp.left_shift(jnp.int32(1), 30 - it)
        return jnp.where(count_ge(cand) >= k, cand, base)
    return lax.fori_loop(0, 31, body, base)


def _cumsum_lanes(x):
    r, w = x.shape
    nb = w // LANE
    xs = jnp.concatenate([x[:, j * LANE:(j + 1) * LANE] for j in range(nb)], axis=0) if nb > 1 else x
    a = lax.broadcasted_iota(jnp.int32, (LANE, LANE), 0)
    b = lax.broadcasted_iota(jnp.int32, (LANE, LANE), 1)
    p = _dot(xs.astype(BF16), jnp.where(a <= b, 1.0, 0.0).astype(BF16))
    if nb == 1:
        return p
    ra = lax.broadcasted_iota(jnp.int32, (nb * r, nb * r), 0)
    rb = lax.broadcasted_iota(jnp.int32, (nb * r, nb * r), 1)
    earlier = jnp.where((rb // r < ra // r) & (rb % r == ra % r), 1.0, 0.0).astype(BF16)
    tot = jnp.broadcast_to(p[:, LANE - 1:LANE], (nb * r, LANE)).astype(BF16)
    p = p + _dot(earlier, tot)
    return jnp.concatenate([p[j * r:(j + 1) * r, :] for j in range(nb)], axis=1)


def _select(keys, kth, need, carry):
    eq = (keys == kth).astype(F32)
    pref = _cumsum_lanes(eq) - eq + carry
    sel = (keys > kth) | ((eq > 0.0) & (pref < need))
    return sel, carry + jnp.sum(eq, axis=-1, keepdims=True)


def _dsa_prompt_kernel(bq_ref, iq_ref, iwt_ref, ik_ref, k_ref, vt_ref, o_ref, keys_ref, sel_ref,
                       *, tq, tkb, nkb, tile0, n_sel):
    i = pl.program_id(1)
    blocks = [slice(kb * tkb, (kb + 1) * tkb) for kb in range(nkb)]
    keyp = (nkb - 1) * tkb + lax.broadcasted_iota(jnp.int32, (tkb, tq), 0)
    visible = keyp <= (tile0 + i) * tq + lax.broadcasted_iota(jnp.int32, (tkb, tq), 1)
    iwt = iwt_ref[...]
    iq = [iq_ref[:, h * DI_B:(h + 1) * DI_B] for h in range(HI_B)]

    for kb, blk in enumerate(blocks):
        ik = ik_ref[blk, :]
        sc = jnp.zeros((tkb, tq), F32)
        for h in range(HI_B):
            sc = sc + jnp.maximum(_dot_nt(ik, iq[h]), 0.0) * iwt[h:h + 1, :]
        if kb == nkb - 1:
            sc = jnp.where(visible, sc, -jnp.inf)
        keys_ref[blk, :] = _score_key(sc)

    def count(pred):
        acc = jnp.zeros((1, tq), F32)
        for blk in blocks:
            acc = acc + jnp.sum(jnp.where(pred(keys_ref[blk, :]), 1.0, 0.0), axis=0, keepdims=True)
        return acc

    kth = _kth_largest(lambda cand: count(lambda x: x >= cand), float(n_sel), (1, tq))
    need = float(n_sel) - count(lambda x: x > kth)

    lower = jnp.where(lax.broadcasted_iota(jnp.int32, (tkb, tkb), 1) < lax.broadcasted_iota(jnp.int32, (tkb, tkb), 0),
                      1.0, 0.0).astype(BF16)
    ties = jnp.zeros((1, tq), F32)
    for kb, blk in enumerate(blocks):
        keys = keys_ref[blk, :]
        eq = keys == kth
        eqf = jnp.where(eq, 1.0, 0.0)
        before = _dot(lower, eqf.astype(BF16)) + ties
        sel = (keys > kth) | (eq & (before < need))
        if kb == nkb - 1:
            sel = sel & visible
        sel_ref[blk, :] = jnp.where(sel, 1.0, 0.0)
        ties = ties + jnp.sum(eqf, axis=0, keepdims=True)

    upper_half = lax.broadcasted_iota(jnp.int32, (tq, LANE), 1) >= HD_B
    outs = [None] * H_B
    for pos in range(H_B):
        qb = bq_ref[:, (pos // 2) * LANE:(pos // 2 + 1) * LANE]
        n = pos % 2
        qpad = jnp.where(upper_half == (n == 1), qb, jnp.zeros_like(qb))
        m = jnp.full((1, tq), NEG, F32)
        l = jnp.zeros((1, tq), F32)
        acc = jnp.zeros((HD_B, tq), F32)
        for blk in blocks:
            sel = sel_ref[blk, :] > 0.0
            s = jnp.where(sel, _dot_nt(k_ref[blk, :], qpad), NEG)
            m_new = jnp.maximum(m, jnp.max(s, axis=0, keepdims=True))
            p = jnp.where(sel, jnp.exp(s - m_new), 0.0)
            alpha = jnp.exp(m - m_new)
            l = alpha * l + jnp.sum(p, axis=0, keepdims=True)
            acc = alpha * acc + _dot(vt_ref[n * HD_B:(n + 1) * HD_B, blk], p.astype(BF16))
            m = m_new
        outs[QHEAD_ORDER[pos]] = acc / l
    o_ref[0] = jnp.concatenate(outs, axis=0).T.astype(BF16)


def _dsa_prompt(bq, iq, iwt, ikb, bkb, bvt, b, t):
    tq = min(128, t)
    tkb = min(512, t)
    nq = t // tq
    tpc = tkb // tq
    n_sel = min(TOPK_MAX, t // 4)
    seq = lambda w: pl.BlockSpec((t, w), lambda bb, i: (bb, 0))
    outs = []
    for cls in range(t // tkb):
        nkb = cls + 1
        qrow = lambda w, cls=cls: pl.BlockSpec((tq, w), lambda bb, i: (bb * nq + cls * tpc + i, 0))
        outs.append(pl.pallas_call(
            functools.partial(_dsa_prompt_kernel, tq=tq, tkb=tkb, nkb=nkb, tile0=cls * tpc, n_sel=n_sel),
            grid=(b, tpc),
            in_specs=[qrow(512), qrow(256),
                      pl.BlockSpec((8, tq), lambda bb, i, cls=cls: (0, bb * nq + cls * tpc + i)),
                      seq(DI_B), seq(128), pl.BlockSpec((128, t), lambda bb, i: (0, bb))],
            out_specs=pl.BlockSpec((1, tq, 512), lambda bb, i: (bb, i, 0)),
            out_shape=jax.ShapeDtypeStruct((b, tkb, 512), BF16),
            scratch_shapes=[pltpu.VMEM((nkb * tkb, tq), jnp.int32), pltpu.VMEM((nkb * tkb, tq), F32)],
            name='dsa_prompt',
            compiler_params=pltpu.CompilerParams(dimension_semantics=('arbitrary', 'arbitrary'),
                                                 vmem_limit_bytes=VMEM_LIMIT),
        )(bq, iq, iwt, ikb, bkb, bvt))
    return jnp.concatenate(outs, axis=1).reshape(b * t, 512)


def _gla_kernel(q_ref, k_ref, v_ref, la_ref, s0_ref, o_ref, sout_ref, st_ref, *, c, nsub):
    j = pl.program_id(1)

    @pl.when(j == 0)
    def _():
        st_ref[...] = s0_ref[0]

    kw = H_C * DK_C
    r = lax.broadcasted_iota(jnp.int32, (c, c), 0)
    cc = lax.broadcasted_iota(jnp.int32, (c, c), 1)
    tril = jnp.where(r >= cc, 1.0, 0.0).astype(BF16)
    row = lax.broadcasted_iota(jnp.int32, (c, kw), 0)
    head = lax.broadcasted_iota(jnp.int32, (c, kw), 1) // DK_C

    def heads(a):
        return [jnp.where(head == h, a, 0.0).astype(BF16) for h in range(H_C)]

    def body(i, carry):
        off = pl.multiple_of(i * c, c)
        la = la_ref[0, pl.ds(off, c), :]
        l1, l2, l3 = _split3(la)
        lb = _dot(tril, l1) + _dot(tril, l2) + _dot(tril, l3)
        q = q_ref[0, pl.ds(off, c), :]
        k = k_ref[0, pl.ds(off, c), :]
        v = v_ref[0, pl.ds(off, c), :]

        k16 = k.astype(BF16)
        att = [jnp.where(r == cc, _dot_nt(qh, k16), 0.0) for qh in heads(q)]
        fl = lb
        b = 1
        while b < c:
            up = (row % (2 * b)) >= b
            if b > 1:
                fl = jnp.where((row % b) >= b // 2, pltpu.roll(fl, b // 2, 0), fl)
            qf = jnp.where(up, q * jnp.exp(jnp.where(up, lb - fl, 0.0)), 0.0)
            nxt = pltpu.roll(fl, c - b, 0)
            kf = jnp.where(up, 0.0, k * jnp.exp(jnp.where(up, 0.0, nxt - lb))).astype(BF16)
            same = (r // (2 * b)) == (cc // (2 * b))
            att = [a + jnp.where(same, _dot_nt(qh, kf), 0.0) for a, qh in zip(att, heads(qf))]
            b *= 2

        st = st_ref[...]
        st16 = st.astype(BF16)
        lb_end = lb[c - 1:c, :]
        upd = jnp.zeros(st.shape, F32)
        for h, (qe, kt) in enumerate(zip(heads(q * jnp.exp(lb)), heads(k * jnp.exp(lb_end - lb)))):
            vh = v[:, h * DV_C:(h + 1) * DV_C].astype(BF16)
            o_ref[0, pl.ds(off, c), h * DV_C:(h + 1) * DV_C] = _dot(att[h].astype(BF16), vh) + _dot_nt(qe, st16)
            upd = upd + _dot_tn(vh, kt)
        st_ref[...] = st * jnp.exp(lb_end) + upd
        return carry
    lax.fori_loop(0, nsub, body, 0)

    @pl.when(j == pl.num_programs(1) - 1)
    def _():
        sout_ref[0] = st_ref[...]


def _gla(cq, ck, cv, la, s0t, b, t):
    c = min(GLA_BLOCK, t)
    tt = min(512, t)
    r3 = lambda a: a.reshape(b, t, a.shape[-1])
    tok = lambda w: pl.BlockSpec((1, tt, w), lambda bb, j: (bb, j, 0))
    st = pl.BlockSpec((1, DV_C, H_C * DK_C), lambda bb, j: (bb, 0, 0))
    oc, s_out = pl.pallas_call(
        functools.partial(_gla_kernel, c=c, nsub=tt // c), grid=(b, t // tt),
        in_specs=[tok(256), tok(256), tok(512), tok(256), st],
        out_specs=[tok(512), st],
        out_shape=[jax.ShapeDtypeStruct((b, t, 512), F32), jax.ShapeDtypeStruct((b, DV_C, H_C * DK_C), F32)],
        scratch_shapes=[pltpu.VMEM((DV_C, H_C * DK_C), F32)], name='gla',
        compiler_params=pltpu.CompilerParams(dimension_semantics=('arbitrary', 'arbitrary'),
                                             vmem_limit_bytes=VMEM_LIMIT),
    )(r3(cq), r3(ck), r3(cv), r3(la), s0t)
    return oc.reshape(b * t, 512), s_out


def _back_merge_kernel(x_ref, oa_ref, ob_ref, oc_ref, cr_ref, gate_ref, goc_ref,
                       wpa_ref, wpb_ref, wpc_ref, wo_ref, y_ref):
    oc, cr = oc_ref[...], cr_ref[...]
    parts = []
    for h in range(H_C):
        och = oc[:, h * DV_C:(h + 1) * DV_C]
        parts.append(och * lax.rsqrt(jnp.mean(och * och, axis=-1, keepdims=True) + EPS))
    ocn = jnp.concatenate(parts, axis=1) * goc_ref[...] * (cr * _sigmoid(cr))
    gate = gate_ref[...]
    merged = (_sigmoid(gate[:, :D_MODEL]) * _dot(oa_ref[...], wpa_ref[...])
              + _sigmoid(gate[:, D_MODEL:2 * D_MODEL]) * _dot(ob_ref[...], wpb_ref[...])
              + _sigmoid(gate[:, 2 * D_MODEL:]) * _dot(ocn.astype(BF16), wpc_ref[...]))
    y_ref[...] = x_ref[...] + _dot(merged.astype(BF16), wo_ref[...])


def _back_merge(x2d, oa, ob, oc, cr, gates, lw, tm):
    n = x2d.shape[0]
    row = lambda w: pl.BlockSpec((tm, w), lambda i: (i, 0))
    full = lambda a: pl.BlockSpec(a.shape, lambda i: (0,) * a.ndim)
    consts = [lw['goc'], lw['w_pa'], lw['w_pb'], lw['w_pc'], lw['w_o']]
    return pl.pallas_call(
        _back_merge_kernel, grid=(n // tm,),
        in_specs=[row(D_MODEL), row(512), row(512), row(512), row(512), row(3 * D_MODEL)]
        + [full(a) for a in consts],
        out_specs=row(D_MODEL), out_shape=jax.ShapeDtypeStruct((n, D_MODEL), F32), name='back_merge',
        compiler_params=pltpu.CompilerParams(dimension_semantics=('arbitrary',), vmem_limit_bytes=VMEM_LIMIT),
    )(x2d, oa, ob, oc, cr, gates, *consts)


def _back_ffn_kernel(x_ref, p_ref, gf_ref, wup_ref, wdn_ref, gp_ref, wpg_ref, wpe_ref, y_ref, xn_ref, acc_ref):
    f = pl.program_id(1)

    @pl.when(f == 0)
    def _():
        xn_ref[...] = _rms(x_ref[...], gf_ref[...]).astype(BF16)
        acc_ref[...] = x_ref[...]

    hid = jnp.maximum(_dot(xn_ref[...], wup_ref[...]), 0.0)
    acc_ref[...] += _dot((hid * hid).astype(BF16), wdn_ref[...])

    @pl.when(f == pl.num_programs(1) - 1)
    def _():
        x2 = acc_ref[...]
        gate = _sigmoid(_dot(_rms(x2, gp_ref[...]).astype(BF16), wpg_ref[...]))
        y_ref[...] = x2 + gate * _dot(p_ref[...].astype(BF16), wpe_ref[...])


def _back_ffn(x2d, p2d, lw, tm):
    n = x2d.shape[0]
    tf = 1024
    row = lambda w: pl.BlockSpec((tm, w), lambda i, f: (i, 0))
    full = lambda a: pl.BlockSpec(a.shape, lambda i, f: (0,) * a.ndim)
    return pl.pallas_call(
        _back_ffn_kernel, grid=(n // tm, D_FF // tf),
        in_specs=[row(D_MODEL), row(D_PLE), full(lw['g_ffn']),
                  pl.BlockSpec((D_MODEL, tf), lambda i, f: (0, f)), pl.BlockSpec((tf, D_MODEL), lambda i, f: (f, 0)),
                  full(lw['g_ple']), full(lw['w_pg']), full(lw['w_pe'])],
        out_specs=row(D_MODEL), out_shape=jax.ShapeDtypeStruct((n, D_MODEL), F32),
        scratch_shapes=[pltpu.VMEM((tm, D_MODEL), BF16), pltpu.VMEM((tm, D_MODEL), F32)], name='back_ffn',
        compiler_params=pltpu.CompilerParams(dimension_semantics=('arbitrary', 'arbitrary'),
                                             vmem_limit_bytes=VMEM_LIMIT),
    )(x2d, p2d, lw['g_ffn'], lw['w_up'], lw['w_down'], lw['g_ple'], lw['w_pg'], lw['w_pe'])


def _mla_decode_kernel(pt_ref, q_ref, iq_ref, iw_ref, cn_ref, krn_ref, kinvn_ref, ikn_ref,
                       wukt_ref, wuv_ref, *rest, npg, cpg, t):
    ckv = rest[:npg]
    krp = rest[npg:2 * npg]
    kix = rest[2 * npg:3 * npg]
    o_ref, keys_ref, keysn_ref, m_ref, l_ref, acc_ref, s_ref, c16_ref = rest[3 * npg:]
    g = pl.program_id(1)
    rows = H_A * t
    nup = H_A * DN_A

    @pl.when(g == 0)
    def _():
        m_ref[...] = jnp.full(m_ref.shape, NEG, F32)
        l_ref[...] = jnp.zeros(l_ref.shape, F32)
        acc_ref[...] = jnp.zeros(acc_ref.shape, F32)

    q = q_ref[0]
    qlat, qr = q[:, :R_KV], q[:, R_KV:R_KV + DR_A]
    lhs = jnp.concatenate([wukt_ref[...], qlat], axis=0)
    iq = iq_ref[0]
    iw = iw_ref[0]

    def softmax_step(s, v16):
        m_prev = m_ref[...]
        m_new = jnp.maximum(m_prev, jnp.max(s, axis=-1, keepdims=True))
        p = jnp.exp(s - m_new)
        alpha = jnp.exp(m_prev - m_new)
        l_ref[...] = alpha * l_ref[...] + jnp.sum(p, axis=-1, keepdims=True)
        acc_ref[...] = alpha * acc_ref[...] + _dot(p.astype(BF16), v16)
        m_ref[...] = m_new

    def per_head(s, kinv_rows):
        return jnp.concatenate([s[h * t:(h + 1) * t, :] * kinv_rows[h] for h in range(H_A)], axis=0)

    def combine_heads(d):
        sc = jnp.zeros((t, d.shape[1]), F32)
        for h in range(HI_B):
            sc = sc + d[h * t:(h + 1) * t, :] * iw[h * t:(h + 1) * t, :1]
        return sc

    for ch in range(npg // cpg):
        pages = range(ch * cpg, (ch + 1) * cpg)
        c16 = jnp.concatenate([ckv[pg][...].astype(BF16) for pg in pages], axis=0)
        krt = jnp.concatenate([krp[pg][...] for pg in pages], axis=1)
        r = _dot_nt(lhs, c16)
        s = r[nup:, :] + _dot(qr, krt.astype(BF16))
        kr2 = jnp.sum(krt * krt, axis=0, keepdims=True)
        kinv = []
        for h in range(H_A):
            knh = r[h * DN_A:(h + 1) * DN_A, :]
            ms = jnp.sum(knh * knh, axis=0, keepdims=True) + kr2
            kinv.append(lax.rsqrt(ms * (1.0 / (DN_A + DR_A)) + EPS))
        cols = slice(ch * cpg * PAGE, (ch + 1) * cpg * PAGE)
        s_ref[:, cols] = per_head(s, kinv)
        c16_ref[cols, :] = c16
        ikt = jnp.concatenate([kix[pg][...].astype(BF16) for pg in pages], axis=1)
        sc = combine_heads(jnp.maximum(_dot(iq, ikt), 0.0))
        keys_ref[0, :, cols] = _score_key(sc)
    softmax_step(s_ref[...], c16_ref[...])

    @pl.when(g == pl.num_programs(1) - 1)
    def _():
        pad = lambda a: jnp.concatenate([a, jnp.zeros((PAGE - t, a.shape[1]), a.dtype)], axis=0)
        col = lax.broadcasted_iota(jnp.int32, (rows, PAGE), 1)
        qt = lax.broadcasted_iota(jnp.int32, (rows, PAGE), 0) % t
        cn16 = pad(cn_ref[0]).astype(BF16)
        kinvn = kinvn_ref[0]
        s = _dot_nt(qlat, cn16) + _dot_nt(qr, pad(krn_ref[0]).astype(BF16))
        s = per_head(s, [kinvn[h:h + 1, :] for h in range(H_A)])
        softmax_step(jnp.where(col <= qt, s, NEG), cn16)
        dn = jnp.maximum(_dot_nt(iq, pad(ikn_ref[0]).astype(BF16)), 0.0)
        scn = jnp.where((col <= qt)[:t], combine_heads(dn), -jnp.inf)
        keysn_ref[0] = _score_key(scn)
        o = acc_ref[...] / l_ref[...]
        olat = jnp.concatenate([o[h * t:(h + 1) * t, :] for h in range(H_A)], axis=1).astype(BF16)
        o_ref[0] = _dot(olat, wuv_ref[...]).astype(BF16)


def _mla_decode(pt_flat, layer, qs, iqs, iws, cn, krn, kinvn, ikn, lw, cache_ckv, krope_t, kidx_t,
                b, t, n_pages):
    npg = min(16, n_pages)
    cpg = min(4, npg)
    ng = n_pages // npg
    past = n_pages * PAGE
    per_b = lambda a: pl.BlockSpec((1,) + a.shape[1:], lambda bb, g, pt: (bb,) + (0,) * (a.ndim - 1))
    full = lambda a: pl.BlockSpec(a.shape, lambda bb, g, pt: (0,) * a.ndim)

    def page(r, w, pg):
        return pl.BlockSpec((None, None, r, w),
                            lambda bb, g, pt: (layer, pt[bb * n_pages + g * npg + pg], 0, 0))
    small = [qs, iqs, iws, cn, krn, kinvn, ikn]
    consts = [lw['wukt'], lw['wuv_bd']]
    in_specs = ([per_b(a) for a in small] + [full(a) for a in consts]
                + [page(PAGE, R_KV, pg) for pg in range(npg)] + [page(DR_A, PAGE, pg) for pg in range(npg)]
                + [page(DI_B, PAGE, pg) for pg in range(npg)])
    grid_spec = pltpu.PrefetchScalarGridSpec(
        num_scalar_prefetch=1, grid=(b, ng), in_specs=in_specs,
        out_specs=[pl.BlockSpec((1, t, 512), lambda bb, g, pt: (bb, 0, 0)),
                   pl.BlockSpec((1, t, npg * PAGE), lambda bb, g, pt: (bb, 0, g)),
                   pl.BlockSpec((1, t, PAGE), lambda bb, g, pt: (bb, 0, 0))],
        scratch_shapes=[pltpu.VMEM((H_A * t, 1), F32), pltpu.VMEM((H_A * t, 1), F32),
                        pltpu.VMEM((H_A * t, R_KV), F32), pltpu.VMEM((H_A * t, npg * PAGE), F32),
                        pltpu.VMEM((npg * PAGE, R_KV), BF16)])
    return pl.pallas_call(
        functools.partial(_mla_decode_kernel, npg=npg, cpg=cpg, t=t), grid_spec=grid_spec,
        out_shape=[jax.ShapeDtypeStruct((b, t, 512), BF16), jax.ShapeDtypeStruct((b, t, past), jnp.int32),
                   jax.ShapeDtypeStruct((b, t, PAGE), jnp.int32)], name='mla_decode',
        compiler_params=pltpu.CompilerParams(dimension_semantics=('arbitrary', 'arbitrary'),
                                             vmem_limit_bytes=VMEM_LIMIT),
    )(pt_flat, *small, *consts, *([cache_ckv] * npg), *([krope_t] * npg), *([kidx_t] * npg))


def _threshold_kernel(keys_ref, keysn_ref, kth_ref, need_ref, tie_ref, *, n_sel):
    nb, t, past = keys_ref.shape
    keys = keys_ref[...].reshape(nb * t, past)
    keysn = keysn_ref[...].reshape(nb * t, PAGE)

    def count(pred):
        return (jnp.sum(jnp.where(pred(keys), 1.0, 0.0), axis=-1, keepdims=True)
                + jnp.sum(jnp.where(pred(keysn), 1.0, 0.0), axis=-1, keepdims=True))
    kth = _kth_largest(lambda cand: count(lambda x: x >= cand), float(n_sel), (nb * t, 1))
    need = float(n_sel) - count(lambda x: x > kth)
    tied = count(lambda x: x == kth)
    wide = lambda a: jnp.broadcast_to(a, (nb * t, LANE)).reshape(nb, t, LANE)
    kth_ref[...] = wide(kth)
    need_ref[...] = wide(need)
    tie_ref[...] = wide(jnp.where(tied > need, 1.0, 0.0))


def _threshold(keys, keysn, n_sel):
    b, t, past = keys.shape
    nb = min(8, b)
    blk = lambda w: pl.BlockSpec((nb, t, w), lambda i: (i, 0, 0))
    return pl.pallas_call(
        functools.partial(_threshold_kernel, n_sel=n_sel), grid=(b // nb,),
        in_specs=[blk(past), blk(PAGE)], out_specs=[blk(LANE)] * 3,
        out_shape=[jax.ShapeDtypeStruct((b, t, LANE), jnp.int32), jax.ShapeDtypeStruct((b, t, LANE), F32),
                   jax.ShapeDtypeStruct((b, t, LANE), F32)], name='topk_threshold',
        compiler_params=pltpu.CompilerParams(dimension_semantics=('arbitrary',), vmem_limit_bytes=VMEM_LIMIT),
    )(keys, keysn)


def _dsa_decode_kernel(pt_ref, tie_ref, keys_ref, keysn_ref, kth_ref, need_ref, bq_ref, kn_ref, vn_ref,
                       *rest, npg, t):
    kp = rest[:npg]
    vp = rest[npg:2 * npg]
    o_ref, carry_ref, sel_ref, m_ref, l_ref, acc_ref = rest[2 * npg:]
    bb, g = pl.program_id(0), pl.program_id(1)
    nk = npg * PAGE
    gsz = H_B // KVH_B

    @pl.when(g == 0)
    def _():
        carry_ref[...] = jnp.zeros(carry_ref.shape, F32)
        m_ref[...] = jnp.full(m_ref.shape, NEG, F32)
        l_ref[...] = jnp.zeros(l_ref.shape, F32)
        acc_ref[...] = jnp.zeros(acc_ref.shape, F32)

    q = bq_ref[0]
    kth = kth_ref[0][:, :1]
    need = need_ref[0][:, :1]
    ties_matter = tie_ref[bb] != 0

    def attend(keys_blk, extra, scores, values):
        w = keys_blk.shape[1]

        @pl.when(ties_matter)
        def _():
            sel, carry = _select(keys_blk, kth, need, carry_ref[...])
            carry_ref[...] = carry
            sel_ref[:, :w] = jnp.where(sel, 1.0, 0.0)

        @pl.when(jnp.logical_not(ties_matter))
        def _():
            sel_ref[:, :w] = jnp.where(keys_blk >= kth, 1.0, 0.0)

        sel = sel_ref[:, :w] > 0.0
        if extra is not None:
            sel = sel & extra
        selg = jnp.concatenate([jnp.where(sel, 1.0, 0.0)] * gsz, axis=0) > 0.0
        for n in range(KVH_B):
            rs = slice(n * gsz * t, (n + 1) * gsz * t)
            s = jnp.where(selg, scores(n, q[rs, :]), NEG)
            m_prev = m_ref[rs, :]
            m_new = jnp.maximum(m_prev, jnp.max(s, axis=-1, keepdims=True))
            p = jnp.where(selg, jnp.exp(s - m_new), 0.0)
            alpha = jnp.exp(m_prev - m_new)
            l_ref[rs, :] = alpha * l_ref[rs, :] + jnp.sum(p, axis=-1, keepdims=True)
            acc_ref[rs, :] = alpha * acc_ref[rs, :] + values(n, p.astype(BF16))
            m_ref[rs, :] = m_new

    kt = [jnp.concatenate([kp[pg][n].astype(BF16) for pg in range(npg)], axis=1) for n in range(KVH_B)]
    vt = [jnp.concatenate([vp[pg][n].astype(BF16) for pg in range(npg)], axis=1) for n in range(KVH_B)]
    off = pl.multiple_of(g * nk, nk)
    attend(keys_ref[0, :, pl.ds(off, nk)], None,
           lambda n, qn: _dot(qn, kt[n]), lambda n, p16: _dot_nt(p16, vt[n]))

    @pl.when(g == pl.num_programs(1) - 1)
    def _():
        pad = lambda a: jnp.concatenate([a, jnp.zeros((PAGE - t, a.shape[1]), a.dtype)], axis=0)
        col = lax.broadcasted_iota(jnp.int32, (t, PAGE), 1)
        qt = lax.broadcasted_iota(jnp.int32, (t, PAGE), 0)
        k16, v16 = pad(kn_ref[0]).astype(BF16), pad(vn_ref[0]).astype(BF16)
        attend(keysn_ref[0], col <= qt,
               lambda n, qn: _dot_nt(qn, k16[:, n * HD_B:(n + 1) * HD_B]),
               lambda n, p16: _dot(p16, v16[:, n * HD_B:(n + 1) * HD_B]))
        o = acc_ref[...] / l_ref[...]
        o_ref[0] = jnp.concatenate([o[h * t:(h + 1) * t, :] for h in range(H_B)], axis=1).astype(BF16)


def _dsa_decode(pt_flat, layer, keys, keysn, bqs, kn, vn, k_t, v_t, b, t, n_pages):
    npg = min(16, n_pages)
    ng = n_pages // npg
    past = n_pages * PAGE
    kth, need, tie = _threshold(keys, keysn, min(TOPK_MAX, (past + t) // 4))
    tie_flag = (jnp.max(tie, axis=(1, 2)) > 0.0).astype(jnp.int32)
    per_b = lambda a: pl.BlockSpec((1,) + a.shape[1:], lambda bb, g, pt, tf: (bb,) + (0,) * (a.ndim - 1))

    def page(pg):
        return pl.BlockSpec((None, None, KVH_B, HD_B, PAGE),
                            lambda bb, g, pt, tf: (layer, pt[bb * n_pages + g * npg + pg], 0, 0, 0))
    small = [keys, keysn, kth, need, bqs, kn, vn]
    grid_spec = pltpu.PrefetchScalarGridSpec(
        num_scalar_prefetch=2, grid=(b, ng),
        in_specs=[per_b(a) for a in small] + [page(pg) for pg in range(npg)] * 2,
        out_specs=pl.BlockSpec((1, t, 512), lambda bb, g, pt, tf: (bb, 0, 0)),
        scratch_shapes=[pltpu.VMEM((t, 1), F32), pltpu.VMEM((t, npg * PAGE), F32),
                        pltpu.VMEM((H_B * t, 1), F32), pltpu.VMEM((H_B * t, 1), F32),
                        pltpu.VMEM((H_B * t, HD_B), F32)])
    return pl.pallas_call(
        functools.partial(_dsa_decode_kernel, npg=npg, t=t), grid_spec=grid_spec,
        out_shape=jax.ShapeDtypeStruct((b, t, 512), BF16), name='dsa_decode',
        compiler_params=pltpu.CompilerParams(dimension_semantics=('arbitrary', 'arbitrary'),
                                             vmem_limit_bytes=VMEM_LIMIT),
    )(pt_flat, tie_flag, *small, *([k_t] * npg), *([v_t] * npg))


def _indicator(width, seg):
    e = np.zeros((width, LANE), np.float32)
    e[np.arange(width), np.arange(width) // seg] = 1.0
    return e


def _prep_layer(w, l):
    pad = lambda a, n: jnp.pad(a, ((0, 0), (0, n - a.shape[1])))
    w_in = w['w_in'][l]
    o_b, o_c, o_g = N_A, N_A + N_B, N_A + N_B + N_C
    w_uq = w['w_uq'][l].reshape(R_Q, H_A, DN_A + DR_A)
    wq_r = jnp.pad(w_uq[:, :, DN_A:], ((0, 0), (0, 0), (0, LANE - DR_A))).reshape(R_Q, H_A * LANE)
    g_q_a = w['g_q_a'][l]
    w_uk = w['w_uk'][l]
    eye = jnp.eye(H_A, dtype=F32)
    wabs = (eye[:, None, :, None] * jnp.transpose(w_uk, (1, 2, 0))[:, :, None, :]).reshape(H_A * DN_A, H_A * R_KV)
    wuv_bd = (eye[:, None, :, None] * jnp.transpose(w['w_uv'][l], (1, 0, 2))[:, :, None, :]
              ).reshape(H_A * R_KV, H_A * DV_A)
    e64 = _indicator(512, 64)
    e128 = _indicator(1024, 128)
    row = lambda a: a.reshape(1, -1)
    nq_b = H_B * HD_B
    wbq = w_in[:, o_b:o_b + nq_b].reshape(D_MODEL, H_B, HD_B)[:, list(QHEAD_ORDER), :].reshape(D_MODEL, nq_b)
    wb = jnp.concatenate([wbq, w_in[:, o_b + nq_b:o_c]], axis=1)
    return dict(
        g_attn=row(w['g_attn'][l]),
        wa=pad(w_in[:, :o_b], W_A).astype(BF16), wb=pad(wb, W_B).astype(BF16),
        wc=pad(w_in[:, o_c:o_g], W_C).astype(BF16), wg=w_in[:, o_g:].astype(BF16),
        g_qa=row(w['g_qa'][l]),
        wq2=jnp.concatenate([w_uq[:, :, :DN_A].reshape(R_Q, H_A * DN_A), wq_r], axis=1).astype(BF16),
        gn=row(jnp.tile(g_q_a[:DN_A], H_A)),
        gr=row(jnp.tile(jnp.pad(g_q_a[DN_A:], (0, LANE - DR_A)), H_A)),
        g_kva=row(w['g_kva'][l]),
        wuk=w_uk.reshape(R_KV, H_A * DN_A).astype(BF16), wukt=w_uk.reshape(R_KV, H_A * DN_A).T.astype(BF16),
        wabs=(wabs * SCALE_A).astype(BF16), wuv_bd=wuv_bd.astype(BF16),
        gqb=row(jnp.tile(w['g_q_b'][l], H_B)), gkb=row(jnp.tile(w['g_k_b'][l], KVH_B)),
        e64=jnp.asarray(e64, BF16), e64t=jnp.asarray(e64.T, BF16),
        e128=jnp.asarray(e128, BF16), e128t=jnp.asarray(e128.T, BF16),
        e8=jnp.asarray(e64.T[:8], BF16),
        wa2=jnp.pad(w['w_a2'][l], ((0, LANE - R_GATE_C), (0, 0))).astype(BF16), b_a=row(w['b_a'][l]),
        goc=row(jnp.tile(w['g_o_c'][l], H_C)),
        w_pa=w['w_pa'][l].astype(BF16), w_pb=w['w_pb'][l].astype(BF16), w_pc=w['w_pc'][l].astype(BF16),
        w_o=w['w_o'][l].astype(BF16), g_ffn=row(w['g_ffn'][l]),
        w_up=w['w_up'][l].astype(BF16), w_down=w['w_down'][l].astype(BF16),
        g_ple=row(w['g_ple'][l]), w_pe=w['w_pe'][l].astype(BF16), w_pg=w['w_pg'][l].astype(BF16),
    )


def _rope_tables(pos):
    def tab(rot, period):
        inv = ROPE_THETA ** (-jnp.arange(0, rot, 2, dtype=F32) / rot)
        ang = pos.astype(F32)[:, None] * inv[None, :]
        c, s = jnp.cos(ang), jnp.sin(ang)
        npass = period - rot
        cos = jnp.concatenate([c, c, jnp.ones((pos.shape[0], npass), F32)], axis=1)
        sin = jnp.concatenate([-s, s, jnp.zeros((pos.shape[0], npass), F32)], axis=1)
        return jnp.tile(cos, (1, LANE // period)), jnp.tile(sin, (1, LANE // period))
    ca, sa = tab(DR_A, LANE)
    cb, sb = tab(ROT_B, HD_B)
    return ca, sa, cb, sb


def _head_major(a, b, t, nh, d):
    return a.reshape(b, t, nh, d).transpose(0, 2, 1, 3).reshape(b, nh * t, d)


def _gla_state_out(st, b):
    return st.reshape(b, DV_C, H_C, DK_C).transpose(0, 2, 3, 1)


def _layer(xp, xs, pp, ps, lw, l, caches, s_prev, pt_flat, tabs_p, tabs_s, dims):
    bp, tp, bs, ts, n_pages = dims
    cache_ckv, krope_t, k_t, v_t, kidx_t = caches
    np_, ns = bp * tp, bs * ts
    tm_p, tm_s = min(256, np_), min(256, ns)

    (q, kcat, c, kr, kinv, bq, bk, bkb, bv, bvt, iq, ik, ikb, iwt) = _front_ab(xp, tabs_p, lw, tm_p)
    cq, ck, cv, la, cr, gates = _front_cg(xp, lw, tm_p)
    oa = _mla_prompt(q, kcat, kinv, lw['wuv_bd'], bp, tp)
    ob = _dsa_prompt(bq, iq, iwt, ikb, bkb, bvt, bp, tp)
    oc, st = _gla(cq, ck, cv, la, jnp.zeros((bp, DV_C, H_C * DK_C), F32), bp, tp)
    x1 = _back_merge(xp, oa, ob, oc, cr, gates, lw, tm_p)
    yp = _back_ffn(x1, pp, lw, min(512, np_))
    rows_p = (c.reshape(bp, tp, R_KV), kr.reshape(bp, tp, DR_A), bk.reshape(bp, tp, KVH_B, HD_B),
              bv.reshape(bp, tp, KVH_B, HD_B), ik.reshape(bp, tp, DI_B), _gla_state_out(st, bp))

    (q, kcat, c, kr, kinv, bq, bk, bkb, bv, bvt, iq, ik, ikb, iwt) = _front_ab(xs, tabs_s, lw, tm_s)
    cq, ck, cv, la, cr, gates = _front_cg(xs, lw, tm_s)
    qs = _head_major(q, bs, ts, H_A, 256)
    iqs = _head_major(iq, bs, ts, HI_B, DI_B)
    iws = jnp.broadcast_to(iwt[:HI_B].reshape(HI_B, bs, ts).transpose(1, 0, 2).reshape(bs, HI_B * ts, 1),
                           (bs, HI_B * ts, LANE))
    kinvn = jnp.pad(kinv.reshape(8, bs, ts).transpose(1, 0, 2), ((0, 0), (0, 0), (0, PAGE - ts)))
    oa, keys, keysn = _mla_decode(pt_flat, l, qs, iqs, iws, c.reshape(bs, ts, R_KV), kr.reshape(bs, ts, DR_A),
                                  kinvn, ik.reshape(bs, ts, DI_B), lw, cache_ckv, krope_t, kidx_t,
                                  bs, ts, n_pages)
    std_order = [QHEAD_ORDER.index(h) for h in range(H_B)]
    bqs = bq.reshape(bs, ts, H_B, HD_B)[:, :, std_order, :].transpose(0, 2, 1, 3).reshape(bs, H_B * ts, HD_B)
    ob = _dsa_decode(pt_flat, l, keys, keysn, bqs, bk.reshape(bs, ts, 128), bv.reshape(bs, ts, 128),
                     k_t, v_t, bs, ts, n_pages)
    s0t = s_prev.transpose(0, 3, 1, 2).reshape(bs, DV_C, H_C * DK_C)
    oc, st = _gla(cq, ck, cv, la, s0t, bs, ts)
    x1 = _back_merge(xs, oa.reshape(ns, 512), ob.reshape(ns, 512), oc, cr, gates, lw, tm_s)
    ys = _back_ffn(x1, ps, lw, min(512, ns))
    rows_s = (c.reshape(bs, ts, R_KV), kr.reshape(bs, ts, DR_A), bk.reshape(bs, ts, KVH_B, HD_B),
              bv.reshape(bs, ts, KVH_B, HD_B), ik.reshape(bs, ts, DI_B), _gla_state_out(st, bs))
    return yp, ys, rows_p, rows_s


def kernel(x_prompt, x_sample, cache_ckv, cache_krope, cache_k, cache_v, cache_kidx, state_gla, page_table,
           p_prompt, p_sample, g_attn, w_in, g_qa, w_uq, g_q_a, g_kva, w_uk, w_uv, g_q_b, g_k_b, w_a2, b_a,
           g_o_c, w_pa, w_pb, w_pc, w_o, g_ffn, w_up, w_down, g_ple, w_pe, w_pg):
    w = dict(g_attn=g_attn, w_in=w_in, g_qa=g_qa, w_uq=w_uq, g_q_a=g_q_a, g_kva=g_kva, w_uk=w_uk, w_uv=w_uv,
             g_q_b=g_q_b, g_k_b=g_k_b, w_a2=w_a2, b_a=b_a, g_o_c=g_o_c, w_pa=w_pa, w_pb=w_pb, w_pc=w_pc,
             w_o=w_o, g_ffn=g_ffn, w_up=w_up, w_down=w_down, g_ple=g_ple, w_pe=w_pe, w_pg=w_pg)
    depth = w_in.shape[0]
    bp, tp, _ = x_prompt.shape
    bs, ts, _ = x_sample.shape
    n_pages = page_table.shape[1]
    past = n_pages * PAGE
    np_, ns = bp * tp, bs * ts
    tabs_p = _rope_tables(jnp.arange(tp))
    tm_s = min(256, ns)
    tabs_s = _rope_tables(past + (jnp.arange(tm_s) % ts))
    pt_flat = page_table.reshape(-1)
    xp, xs = x_prompt.reshape(np_, D_MODEL), x_sample.reshape(ns, D_MODEL)
    caches = (cache_ckv, jnp.swapaxes(cache_krope, 2, 3), jnp.transpose(cache_k, (0, 1, 3, 4, 2)),
              jnp.transpose(cache_v, (0, 1, 3, 4, 2)), jnp.swapaxes(cache_kidx, 2, 3))
    new_p, new_s = [], []
    for l in range(depth):
        lw = _prep_layer(w, l)
        xp, xs, rows_p, rows_s = _layer(
            xp, xs, p_prompt[l].reshape(np_, D_PLE), p_sample[l].reshape(ns, D_PLE), lw, l,
            caches, state_gla[l], pt_flat, tabs_p, tabs_s,
            (bp, tp, bs, ts, n_pages))
        new_p.append(rows_p)
        new_s.append(rows_s)
    stack = lambda rows, i: jnp.stack([r[i] for r in rows], axis=0)
    return ((xp.reshape(bp, tp, D_MODEL), xs.reshape(bs, ts, D_MODEL))
            + tuple(stack(new_p, i) for i in range(6)) + tuple(stack(new_s, i) for i in range(6)))
```

```python
import functools

import numpy as np
import jax
import jax.numpy as jnp
from jax import lax
from jax.experimental import pallas as pl
from jax.experimental.pallas import tpu as pltpu

D_MODEL = 1024
PAGE = 128
H_A, DN_A, DR_A, DV_A, R_Q, R_KV = 8, 64, 32, 64, 256, 128
SCALE_A = (DN_A + DR_A) ** -0.5
H_B, KVH_B, HD_B = 8, 2, 64
ROT_B = HD_B // 4
HI_B, DI_B = 4, 64
TOPK_MAX = 256
H_C, DK_C, DV_C, R_GATE_C = 4, 64, 128, 16
GLA_TAU = 16.0
GLA_BLOCK = 64
DECODE_PAGES = 32
ONES_ROWS = 16
V_ROWS = HD_B + ONES_ROWS
C_ROWS = R_KV + ONES_ROWS
QHEAD_ORDER = (0, 4, 1, 5, 2, 6, 3, 7)
D_FF = 4 * D_MODEL
D_PLE = 256
ROPE_THETA = 500000.0
EPS = 1e-6

N_A = R_Q + R_KV + DR_A
N_B = H_B * HD_B + 2 * KVH_B * HD_B + HI_B * DI_B + DI_B + HI_B
N_C = 2 * H_C * DK_C + 2 * H_C * DV_C + R_GATE_C
W_A, W_B, W_C = 512, 1152, 1664
LANE = 128
NEG = -1e30
INT_MIN = -(2 ** 31)

F32 = jnp.float32
BF16 = jnp.bfloat16
VMEM_LIMIT = 48 * 1024 * 1024


def _dot(a, b):
    return jnp.dot(a, b, preferred_element_type=F32)


def _dot_nt(a, b):
    return lax.dot_general(a, b, (((1,), (1,)), ((), ())), preferred_element_type=F32)


def _dot_tn(a, b):
    return lax.dot_general(a, b, (((0,), (0,)), ((), ())), preferred_element_type=F32)


def _split3(a):
    a1 = a.astype(BF16)
    r1 = a - a1.astype(F32)
    a2 = r1.astype(BF16)
    a3 = (r1 - a2.astype(F32)).astype(BF16)
    return a1, a2, a3


def _dot_f32(a, b_bf):
    a1, a2, a3 = _split3(a)
    return _dot(a1, b_bf) + _dot(a2, b_bf) + _dot(a3, b_bf)


def _rms(x, g):
    return x * lax.rsqrt(jnp.mean(x * x, axis=-1, keepdims=True) + EPS) * g


def _seg_inv(x, e, et, n):
    inv = lax.rsqrt(_dot_f32(x * x, e) * (1.0 / n) + EPS)
    return _dot_f32(inv, et)


def _rope(x, cos, sin, period, half):
    w = x.shape[-1]
    reps = w // LANE
    if reps > 1:
        cos = jnp.concatenate([cos] * reps, axis=1)
        sin = jnp.concatenate([sin] * reps, axis=1)
    lane = lax.broadcasted_iota(jnp.int32, x.shape, 1)
    first = (lane % period) < half
    partner = jnp.where(first, pltpu.roll(x, w - half, 1), pltpu.roll(x, half, 1))
    return x * cos + partner * sin


def _sigmoid(x):
    return 1.0 / (1.0 + jnp.exp(-x))


def _front_ab_kernel(x_ref, g_ref, wa_ref, wb_ref, gqa_ref, wq_ref, gn_ref, gr_ref, gkv_ref, wuk_ref,
                     wabs_ref, gqb_ref, gkb_ref, ca_ref, sa_ref, cb_ref, sb_ref,
                     e64_ref, e64t_ref, e128_ref, e128t_ref, e64x_ref,
                     q_ref, kcat_ref, c_ref, kr_ref, kinv_ref, ct_ref,
                     bq_ref, bk_ref, bkb_ref, bv_ref, bvt_ref, iq_ref, ik_ref, ikb_ref, iwt_ref):
    xn = _rms(x_ref[...], g_ref[...]).astype(BF16)
    za = _dot(xn, wa_ref[...])
    zb = _dot(xn, wb_ref[...])
    ca, sa, cb, sb = ca_ref[...], sa_ref[...], cb_ref[...], sb_ref[...]
    e64, e64t = e64_ref[...], e64t_ref[...]

    aq = _rms(za[:, :R_Q], gqa_ref[...]).astype(BF16)
    q2 = _dot(aq, wq_ref[...])
    qn, qr = q2[:, :512], q2[:, 512:]
    ss = _dot_f32(qn * qn, e64) + _dot_f32(qr * qr, e128_ref[...])
    inv = lax.rsqrt(ss * (1.0 / (DN_A + DR_A)) + EPS)
    qn = qn * _dot_f32(inv, e64t) * gn_ref[...]
    qr = qr * _dot_f32(inv, e128t_ref[...]) * gr_ref[...]
    qr = _rope(qr, ca, sa, LANE, DR_A // 2) * SCALE_A
    qlat = _dot(qn.astype(BF16), wabs_ref[...])
    for h in range(H_A):
        q_ref[:, h * 256:h * 256 + 128] = qlat[:, h * 128:(h + 1) * 128].astype(BF16)
        q_ref[:, h * 256 + 128:(h + 1) * 256] = qr[:, h * 128:(h + 1) * 128].astype(BF16)

    c = _rms(za[:, R_Q:R_Q + R_KV], gkv_ref[...])
    kr = _rope(za[:, R_Q + R_KV:], ca, sa, LANE, DR_A // 2)
    c_ref[...] = c
    kr_ref[...] = kr[:, :DR_A]
    cb16 = c.astype(BF16)
    kcat_ref[:, :R_KV] = cb16
    kcat_ref[:, R_KV:] = kr.astype(BF16)
    ones = jnp.ones((ONES_ROWS, c.shape[0]), F32)
    ct_ref[...] = jnp.concatenate([c.T, ones], axis=0).astype(BF16)
    kn = _dot(cb16, wuk_ref[...])
    ms = _dot_f32(kn * kn, e64x_ref[...]) + jnp.sum(kr * kr, axis=-1, keepdims=True)
    kinv_ref[...] = lax.rsqrt(ms * (1.0 / (DN_A + DR_A)) + EPS)

    bq = zb[:, :512]
    bq = bq * _seg_inv(bq, e64, e64t, HD_B) * gqb_ref[...]
    bq_ref[...] = (_rope(bq, cb, sb, HD_B, ROT_B // 2) * (HD_B ** -0.5)).astype(BF16)
    bk = zb[:, 512:640]
    bk = bk * _seg_inv(bk, e64[:128], e64t[:, :128], HD_B) * gkb_ref[...]
    bk = _rope(bk, cb, sb, HD_B, ROT_B // 2)
    bk_ref[...] = bk
    bkb_ref[...] = bk.astype(BF16)
    bv = zb[:, 640:768]
    bv_ref[...] = bv
    bvt = bv.T
    bvt_ref[...] = jnp.concatenate([bvt[:HD_B], ones, bvt[HD_B:], ones], axis=0).astype(BF16)
    iq_ref[...] = (_rope(zb[:, 768:1024], cb, sb, DI_B, ROT_B // 2) * (DI_B ** -0.5)).astype(BF16)
    last = zb[:, 1024:1152]
    ik = _rope(last, cb, sb, DI_B, ROT_B // 2)[:, :DI_B]
    ik_ref[...] = ik
    ikb_ref[...] = ik.astype(BF16)
    iwt_ref[...] = (last * (HI_B ** -0.5)).T[DI_B:DI_B + 8, :]


def _front_ab(x2d, tabs, lw, tm):
    n = x2d.shape[0]
    nblk = tabs[0].shape[0] // tm
    row = lambda w: pl.BlockSpec((tm, w), lambda i: (i, 0))
    full = lambda a: pl.BlockSpec(a.shape, lambda i: (0,) * a.ndim)
    tab = pl.BlockSpec((tm, LANE), lambda i: (i % nblk, 0))
    consts = [lw['g_attn'], lw['wa'], lw['wb'], lw['g_qa'], lw['wq2'], lw['gn'], lw['gr'], lw['g_kva'],
              lw['wuk'], lw['wabs'], lw['gqb'], lw['gkb']]
    inds = [lw['e64'], lw['e64t'], lw['e128'], lw['e128t'], lw['e64x']]
    outs = [(2048, BF16, None), (256, BF16, None), (R_KV, F32, None), (DR_A, F32, None),
            (H_A * LANE, F32, None), (C_ROWS, BF16),
            (512, BF16, None), (128, F32, None), (128, BF16, None), (128, F32, None), (KVH_B * V_ROWS, BF16),
            (256, BF16, None), (DI_B, F32, None), (DI_B, BF16, None), (8, F32)]
    out_shape, out_specs = [], []
    for o in outs:
        if len(o) == 2:
            out_shape.append(jax.ShapeDtypeStruct((o[0], n), o[1]))
            out_specs.append(pl.BlockSpec((o[0], tm), lambda i: (0, i)))
        else:
            out_shape.append(jax.ShapeDtypeStruct((n, o[0]), o[1]))
            out_specs.append(row(o[0]))
    return pl.pallas_call(
        _front_ab_kernel, grid=(n // tm,),
        in_specs=[row(D_MODEL)] + [full(a) for a in consts] + [tab] * 4 + [full(a) for a in inds],
        out_specs=out_specs, out_shape=out_shape, name='front_ab',
        compiler_params=pltpu.CompilerParams(dimension_semantics=('arbitrary',), vmem_limit_bytes=VMEM_LIMIT),
    )(x2d, *consts, *tabs, *inds)


def _front_cg_kernel(x_ref, g_ref, wc_ref, wg_ref, wa2_ref, ba_ref,
                     cq_ref, ck_ref, cv_ref, la_ref, cr_ref, gate_ref):
    xn = _rms(x_ref[...], g_ref[...]).astype(BF16)
    zc = _dot(xn, wc_ref[...])
    cq_ref[...] = zc[:, :256]
    ck_ref[...] = zc[:, 256:512] * (DK_C ** -0.5)
    cv_ref[...] = zc[:, 512:1024]
    cr_ref[...] = zc[:, 1024:1536]
    u = _dot(zc[:, 1536:1664].astype(BF16), wa2_ref[...]) + ba_ref[...]
    la_ref[...] = (jnp.minimum(u, 0.0) - jnp.log1p(jnp.exp(-jnp.abs(u)))) * (1.0 / GLA_TAU)
    gate_ref[...] = _dot(xn, wg_ref[...])


def _front_cg(x2d, lw, tm):
    n = x2d.shape[0]
    row = lambda w: pl.BlockSpec((tm, w), lambda i: (i, 0))
    full = lambda a: pl.BlockSpec(a.shape, lambda i: (0,) * a.ndim)
    consts = [lw['g_attn'], lw['wc'], lw['wg'], lw['wa2'], lw['b_a']]
    widths = [256, 256, 512, 256, 512, 3 * D_MODEL]
    return pl.pallas_call(
        _front_cg_kernel, grid=(n // tm,),
        in_specs=[row(D_MODEL)] + [full(a) for a in consts],
        out_specs=[row(w) for w in widths],
        out_shape=[jax.ShapeDtypeStruct((n, w), F32) for w in widths], name='front_cg',
        compiler_params=pltpu.CompilerParams(dimension_semantics=('arbitrary',), vmem_limit_bytes=VMEM_LIMIT),
    )(x2d, *consts)


def _mla_prompt_kernel(q_ref, k_ref, ct_ref, kinv_ref, wuv_ref, o_ref, m_ref, acc_ref, *, tq):
    i, j = pl.program_id(1), pl.program_id(2)

    @pl.when(j == 0)
    def _():
        m_ref[...] = jnp.full(m_ref.shape, NEG, F32)
        acc_ref[...] = jnp.zeros(acc_ref.shape, F32)

    def step(diagonal):
        k = k_ref[...]
        ct = ct_ref[...]
        if diagonal:
            visible = (lax.broadcasted_iota(jnp.int32, (tq, tq), 0) <= lax.broadcasted_iota(jnp.int32, (tq, tq), 1))
        def logits(h):
            kinv = kinv_ref[:, h * LANE:(h + 1) * LANE]
            s = _dot_nt(k, q_ref[:, h * 256:(h + 1) * 256]) * jnp.concatenate([kinv] * (tq // LANE), axis=1)
            return jnp.where(visible, s, NEG) if diagonal else s

        s_next = logits(0)
        for h in range(H_A):
            s = s_next
            if h + 1 < H_A:
                s_next = logits(h + 1)
            m_prev = m_ref[h:h + 1, :]
            m_new = jnp.maximum(m_prev, jnp.max(s, axis=0, keepdims=True))
            p = jnp.exp(s - m_new).astype(BF16)
            acc_ref[h] = jnp.exp(m_prev - m_new) * acc_ref[h] + _dot(ct, p)
            m_ref[h:h + 1, :] = m_new

    @pl.when(j < i)
    def _():
        step(False)

    @pl.when(j == i)
    def _():
        step(True)

    @pl.when(j == pl.num_programs(2) - 1)
    def _():
        olat_t = jnp.concatenate([acc_ref[h, :R_KV, :] / acc_ref[h, R_KV:R_KV + 1, :] for h in range(H_A)],
                                 axis=0)
        o_ref[...] = _dot(olat_t.T.astype(BF16), wuv_ref[...]).astype(BF16)


def _mla_prompt(q, kcat, ct, kinv, wuv_bd, b, t):
    tq = min(256, t)
    nq = t // tq
    kblk = lambda bb, i, j: bb * nq + jnp.minimum(j, i)
    return pl.pallas_call(
        functools.partial(_mla_prompt_kernel, tq=tq), grid=(b, nq, nq),
        in_specs=[pl.BlockSpec((tq, 2048), lambda bb, i, j: (bb * nq + i, 0)),
                  pl.BlockSpec((tq, 256), lambda bb, i, j: (kblk(bb, i, j), 0)),
                  pl.BlockSpec((C_ROWS, tq), lambda bb, i, j: (0, kblk(bb, i, j))),
                  pl.BlockSpec((tq, H_A * LANE), lambda bb, i, j: (kblk(bb, i, j), 0)),
                  pl.BlockSpec(wuv_bd.shape, lambda bb, i, j: (0, 0))],
        out_specs=pl.BlockSpec((tq, 512), lambda bb, i, j: (bb * nq + i, 0)),
        out_shape=jax.ShapeDtypeStruct((b * t, 512), BF16),
        scratch_shapes=[pltpu.VMEM((H_A, tq), F32), pltpu.VMEM((H_A, C_ROWS, tq), F32)], name='mla_prompt',
        compiler_params=pltpu.CompilerParams(dimension_semantics=('arbitrary', 'arbitrary', 'arbitrary'),
                                             vmem_limit_bytes=VMEM_LIMIT),
    )(q, kcat, ct, kinv, wuv_bd)


def _score_key(sc):
    sc = jnp.where(sc == 0.0, 0.0, sc)
    bits = lax.bitcast_convert_type(sc, jnp.int32)
    return jnp.where(bits < 0, bits ^ 0x7FFFFFFF, bits)


def _kth_largest(count_ge, k, shape):
    base = jnp.where(count_ge(jnp.zeros(shape, jnp.int32)) >= k, 0, INT_MIN).astype(jnp.int32)

    def body(it, base):
        cand = base + jnp.left_shift(jnp.int32(1), 30 - it)
        return jnp.where(count_ge(cand) >= k, cand, base)
    return lax.fori_loop(0, 31, body, base)


def _cumsum_lanes(x):
    r, w = x.shape
    nb = w // LANE
    xs = jnp.concatenate([x[:, j * LANE:(j + 1) * LANE] for j in range(nb)], axis=0) if nb > 1 else x
    a = lax.broadcasted_iota(jnp.int32, (LANE, LANE), 0)
    b = lax.broadcasted_iota(jnp.int32, (LANE, LANE), 1)
    p = _dot(xs.astype(BF16), jnp.where(a <= b, 1.0, 0.0).astype(BF16))
    if nb == 1:
        return p
    ra = lax.broadcasted_iota(jnp.int32, (nb * r, nb * r), 0)
    rb = lax.broadcasted_iota(jnp.int32, (nb * r, nb * r), 1)
    earlier = jnp.where((rb // r < ra // r) & (rb % r == ra % r), 1.0, 0.0).astype(BF16)
    tot = jnp.broadcast_to(p[:, LANE - 1:LANE], (nb * r, LANE)).astype(BF16)
    p = p + _dot(earlier, tot)
    return jnp.concatenate([p[j * r:(j + 1) * r, :] for j in range(nb)], axis=1)


def _select(keys, kth, need, carry):
    eq = (keys == kth).astype(F32)
    pref = _cumsum_lanes(eq) - eq + carry
    sel = (keys > kth) | ((eq > 0.0) & (pref < need))
    return sel, carry + jnp.sum(eq, axis=-1, keepdims=True)


def _dsa_prompt_kernel(bq_ref, iq_ref, iwt_ref, ik_ref, k_ref, vt_ref, o_ref, keys_ref, bias_ref,
                       *, tq, tkb, nkb, tile0, n_sel):
    i = pl.program_id(1)
    blocks = [slice(kb * tkb, (kb + 1) * tkb) for kb in range(nkb)]
    keyp = (nkb - 1) * tkb + lax.broadcasted_iota(jnp.int32, (tkb, tq), 0)
    visible = keyp <= (tile0 + i) * tq + lax.broadcasted_iota(jnp.int32, (tkb, tq), 1)
    iwt = iwt_ref[...]
    iq = [iq_ref[:, h * DI_B:(h + 1) * DI_B] for h in range(HI_B)]

    for kb, blk in enumerate(blocks):
        ik = ik_ref[blk, :]
        sc = jnp.zeros((tkb, tq), F32)
        for h in range(HI_B):
            sc = sc + jnp.maximum(_dot_nt(ik, iq[h]), 0.0) * iwt[h:h + 1, :]
        if kb == nkb - 1:
            sc = jnp.where(visible, sc, -jnp.inf)
        keys_ref[blk, :] = _score_key(sc)

    def count(pred):
        acc = jnp.zeros((1, tq), F32)
        for blk in blocks:
            acc = acc + jnp.sum(jnp.where(pred(keys_ref[blk, :]), 1.0, 0.0), axis=0, keepdims=True)
        return acc

    kth = _kth_largest(lambda cand: count(lambda x: x >= cand), float(n_sel), (1, tq))
    need = float(n_sel) - count(lambda x: x > kth)

    lower = jnp.where(lax.broadcasted_iota(jnp.int32, (tkb, tkb), 1) < lax.broadcasted_iota(jnp.int32, (tkb, tkb), 0),
                      1.0, 0.0).astype(BF16)
    ties = jnp.zeros((1, tq), F32)
    for kb, blk in enumerate(blocks):
        keys = keys_ref[blk, :]
        eq = keys == kth
        eqf = jnp.where(eq, 1.0, 0.0)
        before = _dot(lower, eqf.astype(BF16)) + ties
        sel = (keys > kth) | (eq & (before < need))
        if kb == nkb - 1:
            sel = sel & visible
        bias_ref[blk, :] = jnp.where(sel, 0.0, NEG)
        ties = ties + jnp.sum(eqf, axis=0, keepdims=True)

    upper_half = lax.broadcasted_iota(jnp.int32, (tq, LANE), 1) >= HD_B

    def logits(pos):
        qb = bq_ref[:, (pos // 2) * LANE:(pos // 2 + 1) * LANE]
        qpad = jnp.where(upper_half == (pos % 2 == 1), qb, jnp.zeros_like(qb))
        return [_dot_nt(k_ref[blk, :], qpad) + bias_ref[blk, :] for blk in blocks]

    outs = [None] * H_B
    s_next = logits(0)
    for pos in range(H_B):
        s = s_next
        if pos + 1 < H_B:
            s_next = logits(pos + 1)
        n = pos % 2
        m = functools.reduce(jnp.maximum, [jnp.max(sb, axis=0, keepdims=True) for sb in s])
        acc = jnp.zeros((V_ROWS, tq), F32)
        for sb, blk in zip(s, blocks):
            acc = acc + _dot(vt_ref[n * V_ROWS:(n + 1) * V_ROWS, blk], jnp.exp(sb - m).astype(BF16))
        outs[QHEAD_ORDER[pos]] = acc[:HD_B, :] / acc[HD_B:HD_B + 1, :]
    o_ref[0] = jnp.concatenate(outs, axis=0).T.astype(BF16)


def _dsa_prompt(bq, iq, iwt, ikb, bkb, bvt, b, t):
    tq = min(128, t)
    tkb = min(512, t)
    nq = t // tq
    tpc = tkb // tq
    n_sel = min(TOPK_MAX, t // 4)
    seq = lambda w: pl.BlockSpec((t, w), lambda bb, i: (bb, 0))
    outs = []
    for cls in range(t // tkb):
        nkb = cls + 1
        qrow = lambda w, cls=cls: pl.BlockSpec((tq, w), lambda bb, i: (bb * nq + cls * tpc + i, 0))
        outs.append(pl.pallas_call(
            functools.partial(_dsa_prompt_kernel, tq=tq, tkb=tkb, nkb=nkb, tile0=cls * tpc, n_sel=n_sel),
            grid=(b, tpc),
            in_specs=[qrow(512), qrow(256),
                      pl.BlockSpec((8, tq), lambda bb, i, cls=cls: (0, bb * nq + cls * tpc + i)),
                      seq(DI_B), seq(128), pl.BlockSpec((KVH_B * V_ROWS, t), lambda bb, i: (0, bb))],
            out_specs=pl.BlockSpec((1, tq, 512), lambda bb, i: (bb, i, 0)),
            out_shape=jax.ShapeDtypeStruct((b, tkb, 512), BF16),
            scratch_shapes=[pltpu.VMEM((nkb * tkb, tq), jnp.int32), pltpu.VMEM((nkb * tkb, tq), F32)],
            name='dsa_prompt',
            compiler_params=pltpu.CompilerParams(dimension_semantics=('arbitrary', 'arbitrary'),
                                                 vmem_limit_bytes=VMEM_LIMIT),
        )(bq, iq, iwt, ikb, bkb, bvt))
    return jnp.concatenate(outs, axis=1).reshape(b * t, 512)


def _gla_kernel(q_ref, k_ref, v_ref, la_ref, s0_ref, o_ref, sout_ref, st_ref, *, c, nsub):
    j = pl.program_id(1)

    @pl.when(j == 0)
    def _():
        st_ref[...] = s0_ref[0]

    kw = H_C * DK_C
    r = lax.broadcasted_iota(jnp.int32, (c, c), 0)
    cc = lax.broadcasted_iota(jnp.int32, (c, c), 1)
    tril = jnp.where(r >= cc, 1.0, 0.0).astype(BF16)
    row = lax.broadcasted_iota(jnp.int32, (c, kw), 0)
    head = lax.broadcasted_iota(jnp.int32, (c, kw), 1) // DK_C

    def heads(a):
        return [jnp.where(head == h, a, 0.0).astype(BF16) for h in range(H_C)]

    def body(i, carry):
        off = pl.multiple_of(i * c, c)
        la = la_ref[0, pl.ds(off, c), :]
        l1, l2, l3 = _split3(la)
        lb = _dot(tril, l1) + _dot(tril, l2) + _dot(tril, l3)
        q = q_ref[0, pl.ds(off, c), :]
        k = k_ref[0, pl.ds(off, c), :]
        v = v_ref[0, pl.ds(off, c), :]

        k16 = k.astype(BF16)
        att = [jnp.where(r == cc, _dot_nt(qh, k16), 0.0) for qh in heads(q)]
        fl = lb
        b = 1
        while b < c:
            up = (row % (2 * b)) >= b
            if b > 1:
                fl = jnp.where((row % b) >= b // 2, pltpu.roll(fl, b // 2, 0), fl)
            qf = jnp.where(up, q * jnp.exp(jnp.where(up, lb - fl, 0.0)), 0.0)
            nxt = pltpu.roll(fl, c - b, 0)
            kf = jnp.where(up, 0.0, k * jnp.exp(jnp.where(up, 0.0, nxt - lb))).astype(BF16)
            same = (r // (2 * b)) == (cc // (2 * b))
            att = [a + jnp.where(same, _dot_nt(qh, kf), 0.0) for a, qh in zip(att, heads(qf))]
            b *= 2

        st = st_ref[...]
        st16 = st.astype(BF16)
        lb_end = lb[c - 1:c, :]
        upd = jnp.zeros(st.shape, F32)
        for h, (qe, kt) in enumerate(zip(heads(q * jnp.exp(lb)), heads(k * jnp.exp(lb_end - lb)))):
            vh = v[:, h * DV_C:(h + 1) * DV_C].astype(BF16)
            o_ref[0, pl.ds(off, c), h * DV_C:(h + 1) * DV_C] = _dot(att[h].astype(BF16), vh) + _dot_nt(qe, st16)
            upd = upd + _dot_tn(vh, kt)
        st_ref[...] = st * jnp.exp(lb_end) + upd
        return carry
    lax.fori_loop(0, nsub, body, 0)

    @pl.when(j == pl.num_programs(1) - 1)
    def _():
        sout_ref[0] = st_ref[...]


def _gla(cq, ck, cv, la, s0t, b, t):
    c = min(GLA_BLOCK, t)
    tt = min(512, t)
    r3 = lambda a: a.reshape(b, t, a.shape[-1])
    tok = lambda w: pl.BlockSpec((1, tt, w), lambda bb, j: (bb, j, 0))
    st = pl.BlockSpec((1, DV_C, H_C * DK_C), lambda bb, j: (bb, 0, 0))
    oc, s_out = pl.pallas_call(
        functools.partial(_gla_kernel, c=c, nsub=tt // c), grid=(b, t // tt),
        in_specs=[tok(256), tok(256), tok(512), tok(256), st],
        out_specs=[tok(512), st],
        out_shape=[jax.ShapeDtypeStruct((b, t, 512), F32), jax.ShapeDtypeStruct((b, DV_C, H_C * DK_C), F32)],
        scratch_shapes=[pltpu.VMEM((DV_C, H_C * DK_C), F32)], name='gla',
        compiler_params=pltpu.CompilerParams(dimension_semantics=('arbitrary', 'arbitrary'),
                                             vmem_limit_bytes=VMEM_LIMIT),
    )(r3(cq), r3(ck), r3(cv), r3(la), s0t)
    return oc.reshape(b * t, 512), s_out


def _back_merge_kernel(x_ref, oa_ref, ob_ref, oc_ref, cr_ref, gate_ref, goc_ref,
                       wpa_ref, wpb_ref, wpc_ref, wo_ref, y_ref):
    oc, cr = oc_ref[...], cr_ref[...]
    parts = []
    for h in range(H_C):
        och = oc[:, h * DV_C:(h + 1) * DV_C]
        parts.append(och * lax.rsqrt(jnp.mean(och * och, axis=-1, keepdims=True) + EPS))
    ocn = jnp.concatenate(parts, axis=1) * goc_ref[...] * (cr * _sigmoid(cr))
    gate = gate_ref[...]
    merged = (_sigmoid(gate[:, :D_MODEL]) * _dot(oa_ref[...], wpa_ref[...])
              + _sigmoid(gate[:, D_MODEL:2 * D_MODEL]) * _dot(ob_ref[...], wpb_ref[...])
              + _sigmoid(gate[:, 2 * D_MODEL:]) * _dot(ocn.astype(BF16), wpc_ref[...]))
    y_ref[...] = x_ref[...] + _dot(merged.astype(BF16), wo_ref[...])


def _back_merge(x2d, oa, ob, oc, cr, gates, lw, tm):
    n = x2d.shape[0]
    row = lambda w: pl.BlockSpec((tm, w), lambda i: (i, 0))
    full = lambda a: pl.BlockSpec(a.shape, lambda i: (0,) * a.ndim)
    consts = [lw['goc'], lw['w_pa'], lw['w_pb'], lw['w_pc'], lw['w_o']]
    return pl.pallas_call(
        _back_merge_kernel, grid=(n // tm,),
        in_specs=[row(D_MODEL), row(512), row(512), row(512), row(512), row(3 * D_MODEL)]
        + [full(a) for a in consts],
        out_specs=row(D_MODEL), out_shape=jax.ShapeDtypeStruct((n, D_MODEL), F32), name='back_merge',
        compiler_params=pltpu.CompilerParams(dimension_semantics=('arbitrary',), vmem_limit_bytes=VMEM_LIMIT),
    )(x2d, oa, ob, oc, cr, gates, *consts)


def _back_ffn_kernel(x_ref, p_ref, gf_ref, wup_ref, wdn_ref, gp_ref, wpg_ref, wpe_ref, y_ref, xn_ref, acc_ref):
    f = pl.program_id(1)

    @pl.when(f == 0)
    def _():
        xn_ref[...] = _rms(x_ref[...], gf_ref[...]).astype(BF16)
        acc_ref[...] = x_ref[...]

    hid = jnp.maximum(_dot(xn_ref[...], wup_ref[...]), 0.0)
    acc_ref[...] += _dot((hid * hid).astype(BF16), wdn_ref[...])

    @pl.when(f == pl.num_programs(1) - 1)
    def _():
        x2 = acc_ref[...]
        gate = _sigmoid(_dot(_rms(x2, gp_ref[...]).astype(BF16), wpg_ref[...]))
        y_ref[...] = x2 + gate * _dot(p_ref[...].astype(BF16), wpe_ref[...])


def _back_ffn(x2d, p2d, lw, tm):
    n = x2d.shape[0]
    tf = 1024
    row = lambda w: pl.BlockSpec((tm, w), lambda i, f: (i, 0))
    full = lambda a: pl.BlockSpec(a.shape, lambda i, f: (0,) * a.ndim)
    return pl.pallas_call(
        _back_ffn_kernel, grid=(n // tm, D_FF // tf),
        in_specs=[row(D_MODEL), row(D_PLE), full(lw['g_ffn']),
                  pl.BlockSpec((D_MODEL, tf), lambda i, f: (0, f)), pl.BlockSpec((tf, D_MODEL), lambda i, f: (f, 0)),
                  full(lw['g_ple']), full(lw['w_pg']), full(lw['w_pe'])],
        out_specs=row(D_MODEL), out_shape=jax.ShapeDtypeStruct((n, D_MODEL), F32),
        scratch_shapes=[pltpu.VMEM((tm, D_MODEL), BF16), pltpu.VMEM((tm, D_MODEL), F32)], name='back_ffn',
        compiler_params=pltpu.CompilerParams(dimension_semantics=('arbitrary', 'arbitrary'),
                                             vmem_limit_bytes=VMEM_LIMIT),
    )(x2d, p2d, lw['g_ffn'], lw['w_up'], lw['w_down'], lw['g_ple'], lw['w_pg'], lw['w_pe'])


def _mla_decode_kernel(pt_ref, q_ref, iq_ref, iw_ref, cn_ref, krn_ref, kinvn_ref, ikn_ref,
                       wukt_ref, wuv_ref, *rest, npg, cpg, t):
    ckv = rest[:npg]
    krp = rest[npg:2 * npg]
    kix = rest[2 * npg:3 * npg]
    o_ref, keys_ref, keysn_ref, m_ref, l_ref, acc_ref, s_ref, c16_ref = rest[3 * npg:]
    g = pl.program_id(1)
    rows = H_A * t
    nup = H_A * DN_A

    @pl.when(g == 0)
    def _():
        m_ref[...] = jnp.full(m_ref.shape, NEG, F32)
        l_ref[...] = jnp.zeros(l_ref.shape, F32)
        acc_ref[...] = jnp.zeros(acc_ref.shape, F32)

    q = q_ref[0]
    qlat, qr = q[:, :R_KV], q[:, R_KV:R_KV + DR_A]
    lhs = jnp.concatenate([wukt_ref[...], qlat], axis=0)
    iq = iq_ref[0]
    iw = iw_ref[0]

    def softmax_step(s, v16):
        m_prev = m_ref[...]
        m_new = jnp.maximum(m_prev, jnp.max(s, axis=-1, keepdims=True))
        p = jnp.exp(s - m_new)
        alpha = jnp.exp(m_prev - m_new)
        l_ref[...] = alpha * l_ref[...] + jnp.sum(p, axis=-1, keepdims=True)
        acc_ref[...] = alpha * acc_ref[...] + _dot(p.astype(BF16), v16)
        m_ref[...] = m_new

    def per_head(s, kinv_rows):
        return jnp.concatenate([s[h * t:(h + 1) * t, :] * kinv_rows[h] for h in range(H_A)], axis=0)

    def combine_heads(d):
        sc = jnp.zeros((t, d.shape[1]), F32)
        for h in range(HI_B):
            sc = sc + d[h * t:(h + 1) * t, :] * iw[h * t:(h + 1) * t, :1]
        return sc

    for ch in range(npg // cpg):
        pages = range(ch * cpg, (ch + 1) * cpg)
        c16 = jnp.concatenate([ckv[pg][...].astype(BF16) for pg in pages], axis=0)
        krt = jnp.concatenate([krp[pg][...] for pg in pages], axis=1)
        r = _dot_nt(lhs, c16)
        s = r[nup:, :] + _dot(qr, krt.astype(BF16))
        kr2 = jnp.sum(krt * krt, axis=0, keepdims=True)
        kinv = []
        for h in range(H_A):
            knh = r[h * DN_A:(h + 1) * DN_A, :]
            ms = jnp.sum(knh * knh, axis=0, keepdims=True) + kr2
            kinv.append(lax.rsqrt(ms * (1.0 / (DN_A + DR_A)) + EPS))
        cols = slice(ch * cpg * PAGE, (ch + 1) * cpg * PAGE)
        s_ref[:, cols] = per_head(s, kinv)
        c16_ref[cols, :] = c16
        ikt = jnp.concatenate([kix[pg][...].astype(BF16) for pg in pages], axis=1)
        sc = combine_heads(jnp.maximum(_dot(iq, ikt), 0.0))
        keys_ref[0, :, cols] = _score_key(sc)
    softmax_step(s_ref[...], c16_ref[...])

    @pl.when(g == pl.num_programs(1) - 1)
    def _():
        pad = lambda a: jnp.concatenate([a, jnp.zeros((PAGE - t, a.shape[1]), a.dtype)], axis=0)
        col = lax.broadcasted_iota(jnp.int32, (rows, PAGE), 1)
        qt = lax.broadcasted_iota(jnp.int32, (rows, PAGE), 0) % t
        cn16 = pad(cn_ref[0]).astype(BF16)
        kinvn = kinvn_ref[0]
        s = _dot_nt(qlat, cn16) + _dot_nt(qr, pad(krn_ref[0]).astype(BF16))
        s = per_head(s, [kinvn[h:h + 1, :] for h in range(H_A)])
        softmax_step(jnp.where(col <= qt, s, NEG), cn16)
        dn = jnp.maximum(_dot_nt(iq, pad(ikn_ref[0]).astype(BF16)), 0.0)
        scn = jnp.where((col <= qt)[:t], combine_heads(dn), -jnp.inf)
        keysn_ref[0] = _score_key(scn)
        o = acc_ref[...] / l_ref[...]
        olat = jnp.concatenate([o[h * t:(h + 1) * t, :] for h in range(H_A)], axis=1).astype(BF16)
        o_ref[0] = _dot(olat, wuv_ref[...]).astype(BF16)


def _mla_decode(pt_flat, layer, qs, iqs, iws, cn, krn, kinvn, ikn, lw, cache_ckv, krope_t, kidx_t,
                b, t, n_pages):
    npg = min(DECODE_PAGES, n_pages)
    cpg = min(4, npg)
    ng = n_pages // npg
    past = n_pages * PAGE
    per_b = lambda a: pl.BlockSpec((1,) + a.shape[1:], lambda bb, g, pt: (bb,) + (0,) * (a.ndim - 1))
    full = lambda a: pl.BlockSpec(a.shape, lambda bb, g, pt: (0,) * a.ndim)

    def page(r, w, pg):
        return pl.BlockSpec((None, None, r, w),
                            lambda bb, g, pt: (layer, pt[bb * n_pages + g * npg + pg], 0, 0))
    small = [qs, iqs, iws, cn, krn, kinvn, ikn]
    consts = [lw['wukt'], lw['wuv_bd']]
    in_specs = ([per_b(a) for a in small] + [full(a) for a in consts]
                + [page(PAGE, R_KV, pg) for pg in range(npg)] + [page(DR_A, PAGE, pg) for pg in range(npg)]
                + [page(DI_B, PAGE, pg) for pg in range(npg)])
    grid_spec = pltpu.PrefetchScalarGridSpec(
        num_scalar_prefetch=1, grid=(b, ng), in_specs=in_specs,
        out_specs=[pl.BlockSpec((1, t, 512), lambda bb, g, pt: (bb, 0, 0)),
                   pl.BlockSpec((1, t, npg * PAGE), lambda bb, g, pt: (bb, 0, g)),
                   pl.BlockSpec((1, t, PAGE), lambda bb, g, pt: (bb, 0, 0))],
        scratch_shapes=[pltpu.VMEM((H_A * t, 1), F32), pltpu.VMEM((H_A * t, 1), F32),
                        pltpu.VMEM((H_A * t, R_KV), F32), pltpu.VMEM((H_A * t, npg * PAGE), F32),
                        pltpu.VMEM((npg * PAGE, R_KV), BF16)])
    return pl.pallas_call(
        functools.partial(_mla_decode_kernel, npg=npg, cpg=cpg, t=t), grid_spec=grid_spec,
        out_shape=[jax.ShapeDtypeStruct((b, t, 512), BF16), jax.ShapeDtypeStruct((b, t, past), jnp.int32),
                   jax.ShapeDtypeStruct((b, t, PAGE), jnp.int32)], name='mla_decode',
        compiler_params=pltpu.CompilerParams(dimension_semantics=('arbitrary', 'arbitrary'),
                                             vmem_limit_bytes=VMEM_LIMIT),
    )(pt_flat, *small, *consts, *([cache_ckv] * npg), *([krope_t] * npg), *([kidx_t] * npg))


def _threshold_kernel(keys_ref, keysn_ref, kth_ref, need_ref, tie_ref, *, n_sel):
    nb, t, past = keys_ref.shape
    keys = keys_ref[...].reshape(nb * t, past)
    keysn = keysn_ref[...].reshape(nb * t, PAGE)

    def count(pred):
        return (jnp.sum(jnp.where(pred(keys), 1.0, 0.0), axis=-1, keepdims=True)
                + jnp.sum(jnp.where(pred(keysn), 1.0, 0.0), axis=-1, keepdims=True))
    kth = _kth_largest(lambda cand: count(lambda x: x >= cand), float(n_sel), (nb * t, 1))
    need = float(n_sel) - count(lambda x: x > kth)
    tied = count(lambda x: x == kth)
    wide = lambda a: jnp.broadcast_to(a, (nb * t, LANE)).reshape(nb, t, LANE)
    kth_ref[...] = wide(kth)
    need_ref[...] = wide(need)
    tie_ref[...] = wide(jnp.where(tied > need, 1.0, 0.0))


def _threshold(keys, keysn, n_sel):
    b, t, past = keys.shape
    nb = min(8, b)
    blk = lambda w: pl.BlockSpec((nb, t, w), lambda i: (i, 0, 0))
    return pl.pallas_call(
        functools.partial(_threshold_kernel, n_sel=n_sel), grid=(b // nb,),
        in_specs=[blk(past), blk(PAGE)], out_specs=[blk(LANE)] * 3,
        out_shape=[jax.ShapeDtypeStruct((b, t, LANE), jnp.int32), jax.ShapeDtypeStruct((b, t, LANE), F32),
                   jax.ShapeDtypeStruct((b, t, LANE), F32)], name='topk_threshold',
        compiler_params=pltpu.CompilerParams(dimension_semantics=('arbitrary',), vmem_limit_bytes=VMEM_LIMIT),
    )(keys, keysn)


def _dsa_decode_kernel(pt_ref, tie_ref, keys_ref, keysn_ref, kth_ref, need_ref, bq_ref, kn_ref, vn_ref,
                       *rest, npg, t):
    kp = rest[:npg]
    vp = rest[npg:2 * npg]
    o_ref, carry_ref, bias_ref, m_ref, l_ref, acc_ref = rest[2 * npg:]
    bb, g = pl.program_id(0), pl.program_id(1)
    nk = npg * PAGE
    gsz = H_B // KVH_B

    @pl.when(g == 0)
    def _():
        carry_ref[...] = jnp.zeros(carry_ref.shape, F32)
        m_ref[...] = jnp.full(m_ref.shape, NEG, F32)
        l_ref[...] = jnp.zeros(l_ref.shape, F32)
        acc_ref[...] = jnp.zeros(acc_ref.shape, F32)

    q = bq_ref[0]
    kth = kth_ref[0][:, :1]
    need = need_ref[0][:, :1]
    ties_matter = tie_ref[bb] != 0

    def attend(keys_blk, extra, scores, values):
        w = keys_blk.shape[1]

        visible = True if extra is None else extra

        @pl.when(ties_matter)
        def _():
            sel, carry = _select(keys_blk, kth, need, carry_ref[...])
            carry_ref[...] = carry
            bias_ref[:, :w] = jnp.where(sel & visible, 0.0, NEG)

        @pl.when(jnp.logical_not(ties_matter))
        def _():
            bias_ref[:, :w] = jnp.where((keys_blk >= kth) & visible, 0.0, NEG)

        rs = [slice(n * gsz * t, (n + 1) * gsz * t) for n in range(KVH_B)]
        s = jnp.concatenate([scores(n, q[rs[n], :]) for n in range(KVH_B)], axis=0)
        s = s + jnp.concatenate([bias_ref[:, :w]] * H_B, axis=0)
        m_prev = m_ref[...]
        m_new = jnp.maximum(m_prev, jnp.max(s, axis=-1, keepdims=True))
        p = jnp.exp(s - m_new)
        alpha = jnp.exp(m_prev - m_new)
        l_ref[...] = alpha * l_ref[...] + jnp.sum(p, axis=-1, keepdims=True)
        pv = jnp.concatenate([values(n, p[rs[n], :].astype(BF16)) for n in range(KVH_B)], axis=0)
        acc_ref[...] = alpha * acc_ref[...] + pv
        m_ref[...] = m_new

    kt = [jnp.concatenate([kp[pg][n].astype(BF16) for pg in range(npg)], axis=1) for n in range(KVH_B)]
    vt = [jnp.concatenate([vp[pg][n].astype(BF16) for pg in range(npg)], axis=1) for n in range(KVH_B)]
    off = pl.multiple_of(g * nk, nk)
    attend(keys_ref[0, :, pl.ds(off, nk)], None,
           lambda n, qn: _dot(qn, kt[n]), lambda n, p16: _dot_nt(p16, vt[n]))

    @pl.when(g == pl.num_programs(1) - 1)
    def _():
        pad = lambda a: jnp.concatenate([a, jnp.zeros((PAGE - t, a.shape[1]), a.dtype)], axis=0)
        col = lax.broadcasted_iota(jnp.int32, (t, PAGE), 1)
        qt = lax.broadcasted_iota(jnp.int32, (t, PAGE), 0)
        k16, v16 = pad(kn_ref[0]).astype(BF16), pad(vn_ref[0]).astype(BF16)
        attend(keysn_ref[0], col <= qt,
               lambda n, qn: _dot_nt(qn, k16[:, n * HD_B:(n + 1) * HD_B]),
               lambda n, p16: _dot(p16, v16[:, n * HD_B:(n + 1) * HD_B]))
        o = acc_ref[...] / l_ref[...]
        o_ref[0] = jnp.concatenate([o[h * t:(h + 1) * t, :] for h in range(H_B)], axis=1).astype(BF16)


def _dsa_decode(pt_flat, layer, keys, keysn, bqs, kn, vn, k_t, v_t, b, t, n_pages):
    npg = min(DECODE_PAGES, n_pages)
    ng = n_pages // npg
    past = n_pages * PAGE
    kth, need, tie = _threshold(keys, keysn, min(TOPK_MAX, (past + t) // 4))
    tie_flag = (jnp.max(tie, axis=(1, 2)) > 0.0).astype(jnp.int32)
    per_b = lambda a: pl.BlockSpec((1,) + a.shape[1:], lambda bb, g, pt, tf: (bb,) + (0,) * (a.ndim - 1))

    def page(pg):
        return pl.BlockSpec((None, None, KVH_B, HD_B, PAGE),
                            lambda bb, g, pt, tf: (layer, pt[bb * n_pages + g * npg + pg], 0, 0, 0))
    small = [keys, keysn, kth, need, bqs, kn, vn]
    grid_spec = pltpu.PrefetchScalarGridSpec(
        num_scalar_prefetch=2, grid=(b, ng),
        in_specs=[per_b(a) for a in small] + [page(pg) for pg in range(npg)] * 2,
        out_specs=pl.BlockSpec((1, t, 512), lambda bb, g, pt, tf: (bb, 0, 0)),
        scratch_shapes=[pltpu.VMEM((t, 1), F32), pltpu.VMEM((t, npg * PAGE), F32),
                        pltpu.VMEM((H_B * t, 1), F32), pltpu.VMEM((H_B * t, 1), F32),
                        pltpu.VMEM((H_B * t, HD_B), F32)])
    return pl.pallas_call(
        functools.partial(_dsa_decode_kernel, npg=npg, t=t), grid_spec=grid_spec,
        out_shape=jax.ShapeDtypeStruct((b, t, 512), BF16), name='dsa_decode',
        compiler_params=pltpu.CompilerParams(dimension_semantics=('arbitrary', 'arbitrary'),
                                             vmem_limit_bytes=VMEM_LIMIT),
    )(pt_flat, tie_flag, *small, *([k_t] * npg), *([v_t] * npg))


def _indicator(width, seg):
    e = np.zeros((width, LANE), np.float32)
    e[np.arange(width), np.arange(width) // seg] = 1.0
    return e


def _prep_layer(w, l):
    pad = lambda a, n: jnp.pad(a, ((0, 0), (0, n - a.shape[1])))
    w_in = w['w_in'][l]
    o_b, o_c, o_g = N_A, N_A + N_B, N_A + N_B + N_C
    w_uq = w['w_uq'][l].reshape(R_Q, H_A, DN_A + DR_A)
    wq_r = jnp.pad(w_uq[:, :, DN_A:], ((0, 0), (0, 0), (0, LANE - DR_A))).reshape(R_Q, H_A * LANE)
    g_q_a = w['g_q_a'][l]
    w_uk = w['w_uk'][l]
    eye = jnp.eye(H_A, dtype=F32)
    wabs = (eye[:, None, :, None] * jnp.transpose(w_uk, (1, 2, 0))[:, :, None, :]).reshape(H_A * DN_A, H_A * R_KV)
    wuv_bd = (eye[:, None, :, None] * jnp.transpose(w['w_uv'][l], (1, 0, 2))[:, :, None, :]
              ).reshape(H_A * R_KV, H_A * DV_A)
    e64 = _indicator(512, 64)
    e128 = _indicator(1024, 128)
    row = lambda a: a.reshape(1, -1)
    nq_b = H_B * HD_B
    wbq = w_in[:, o_b:o_b + nq_b].reshape(D_MODEL, H_B, HD_B)[:, list(QHEAD_ORDER), :].reshape(D_MODEL, nq_b)
    wb = jnp.concatenate([wbq, w_in[:, o_b + nq_b:o_c]], axis=1)
    return dict(
        g_attn=row(w['g_attn'][l]),
        wa=pad(w_in[:, :o_b], W_A).astype(BF16), wb=pad(wb, W_B).astype(BF16),
        wc=pad(w_in[:, o_c:o_g], W_C).astype(BF16), wg=w_in[:, o_g:].astype(BF16),
        g_qa=row(w['g_qa'][l]),
        wq2=jnp.concatenate([w_uq[:, :, :DN_A].reshape(R_Q, H_A * DN_A), wq_r], axis=1).astype(BF16),
        gn=row(jnp.tile(g_q_a[:DN_A], H_A)),
        gr=row(jnp.tile(jnp.pad(g_q_a[DN_A:], (0, LANE - DR_A)), H_A)),
        g_kva=row(w['g_kva'][l]),
        wuk=w_uk.reshape(R_KV, H_A * DN_A).astype(BF16), wukt=w_uk.reshape(R_KV, H_A * DN_A).T.astype(BF16),
        wabs=(wabs * SCALE_A).astype(BF16), wuv_bd=wuv_bd.astype(BF16),
        gqb=row(jnp.tile(w['g_q_b'][l], H_B)), gkb=row(jnp.tile(w['g_k_b'][l], KVH_B)),
        e64=jnp.asarray(e64, BF16), e64t=jnp.asarray(e64.T, BF16),
        e128=jnp.asarray(e128, BF16), e128t=jnp.asarray(e128.T, BF16),
        e64x=jnp.asarray(np.kron(np.eye(H_A, dtype=np.float32), np.ones((DN_A, LANE), np.float32)), BF16),
        wa2=jnp.pad(w['w_a2'][l], ((0, LANE - R_GATE_C), (0, 0))).astype(BF16), b_a=row(w['b_a'][l]),
        goc=row(jnp.tile(w['g_o_c'][l], H_C)),
        w_pa=w['w_pa'][l].astype(BF16), w_pb=w['w_pb'][l].astype(BF16), w_pc=w['w_pc'][l].astype(BF16),
        w_o=w['w_o'][l].astype(BF16), g_ffn=row(w['g_ffn'][l]),
        w_up=w['w_up'][l].astype(BF16), w_down=w['w_down'][l].astype(BF16),
        g_ple=row(w['g_ple'][l]), w_pe=w['w_pe'][l].astype(BF16), w_pg=w['w_pg'][l].astype(BF16),
    )


def _rope_tables(pos):
    def tab(rot, period):
        inv = ROPE_THETA ** (-jnp.arange(0, rot, 2, dtype=F32) / rot)
        ang = pos.astype(F32)[:, None] * inv[None, :]
        c, s = jnp.cos(ang), jnp.sin(ang)
        npass = period - rot
        cos = jnp.concatenate([c, c, jnp.ones((pos.shape[0], npass), F32)], axis=1)
        sin = jnp.concatenate([-s, s, jnp.zeros((pos.shape[0], npass), F32)], axis=1)
        return jnp.tile(cos, (1, LANE // period)), jnp.tile(sin, (1, LANE // period))
    ca, sa = tab(DR_A, LANE)
    cb, sb = tab(ROT_B, HD_B)
    return ca, sa, cb, sb


def _head_major(a, b, t, nh, d):
    return a.reshape(b, t, nh, d).transpose(0, 2, 1, 3).reshape(b, nh * t, d)


def _gla_state_out(st, b):
    return st.reshape(b, DV_C, H_C, DK_C).transpose(0, 2, 3, 1)


def _layer(xp, xs, pp, ps, lw, l, caches, s_prev, pt_flat, tabs_p, tabs_s, dims):
    bp, tp, bs, ts, n_pages = dims
    cache_ckv, krope_t, k_t, v_t, kidx_t = caches
    np_, ns = bp * tp, bs * ts
    tm_p, tm_s = min(256, np_), min(256, ns)

    (q, kcat, c, kr, kinv, ct, bq, bk, bkb, bv, bvt, iq, ik, ikb, iwt) = _front_ab(xp, tabs_p, lw, tm_p)
    cq, ck, cv, la, cr, gates = _front_cg(xp, lw, tm_p)
    oa = _mla_prompt(q, kcat, ct, kinv, lw['wuv_bd'], bp, tp)
    ob = _dsa_prompt(bq, iq, iwt, ikb, bkb, bvt, bp, tp)
    oc, st = _gla(cq, ck, cv, la, jnp.zeros((bp, DV_C, H_C * DK_C), F32), bp, tp)
    x1 = _back_merge(xp, oa, ob, oc, cr, gates, lw, tm_p)
    yp = _back_ffn(x1, pp, lw, min(512, np_))
    rows_p = (c.reshape(bp, tp, R_KV), kr.reshape(bp, tp, DR_A), bk.reshape(bp, tp, KVH_B, HD_B),
              bv.reshape(bp, tp, KVH_B, HD_B), ik.reshape(bp, tp, DI_B), _gla_state_out(st, bp))

    (q, kcat, c, kr, kinv, ct, bq, bk, bkb, bv, bvt, iq, ik, ikb, iwt) = _front_ab(xs, tabs_s, lw, tm_s)
    cq, ck, cv, la, cr, gates = _front_cg(xs, lw, tm_s)
    qs = _head_major(q, bs, ts, H_A, 256)
    iqs = _head_major(iq, bs, ts, HI_B, DI_B)
    iws = jnp.broadcast_to(iwt[:HI_B].reshape(HI_B, bs, ts).transpose(1, 0, 2).reshape(bs, HI_B * ts, 1),
                           (bs, HI_B * ts, LANE))
    kinvn = jnp.pad(kinv[:, ::LANE].reshape(bs, ts, H_A).transpose(0, 2, 1), ((0, 0), (0, 0), (0, PAGE - ts)))
    oa, keys, keysn = _mla_decode(pt_flat, l, qs, iqs, iws, c.reshape(bs, ts, R_KV), kr.reshape(bs, ts, DR_A),
                                  kinvn, ik.reshape(bs, ts, DI_B), lw, cache_ckv, krope_t, kidx_t,
                                  bs, ts, n_pages)
    std_order = [QHEAD_ORDER.index(h) for h in range(H_B)]
    bqs = bq.reshape(bs, ts, H_B, HD_B)[:, :, std_order, :].transpose(0, 2, 1, 3).reshape(bs, H_B * ts, HD_B)
    ob = _dsa_decode(pt_flat, l, keys, keysn, bqs, bk.reshape(bs, ts, 128), bv.reshape(bs, ts, 128),
                     k_t, v_t, bs, ts, n_pages)
    s0t = s_prev.transpose(0, 3, 1, 2).reshape(bs, DV_C, H_C * DK_C)
    oc, st = _gla(cq, ck, cv, la, s0t, bs, ts)
    x1 = _back_merge(xs, oa.reshape(ns, 512), ob.reshape(ns, 512), oc, cr, gates, lw, tm_s)
    ys = _back_ffn(x1, ps, lw, min(512, ns))
    rows_s = (c.reshape(bs, ts, R_KV), kr.reshape(bs, ts, DR_A), bk.reshape(bs, ts, KVH_B, HD_B),
              bv.reshape(bs, ts, KVH_B, HD_B), ik.reshape(bs, ts, DI_B), _gla_state_out(st, bs))
    return yp, ys, rows_p, rows_s


def kernel(x_prompt, x_sample, cache_ckv, cache_krope, cache_k, cache_v, cache_kidx, state_gla, page_table,
           p_prompt, p_sample, g_attn, w_in, g_qa, w_uq, g_q_a, g_kva, w_uk, w_uv, g_q_b, g_k_b, w_a2, b_a,
           g_o_c, w_pa, w_pb, w_pc, w_o, g_ffn, w_up, w_down, g_ple, w_pe, w_pg):
    w = dict(g_attn=g_attn, w_in=w_in, g_qa=g_qa, w_uq=w_uq, g_q_a=g_q_a, g_kva=g_kva, w_uk=w_uk, w_uv=w_uv,
             g_q_b=g_q_b, g_k_b=g_k_b, w_a2=w_a2, b_a=b_a, g_o_c=g_o_c, w_pa=w_pa, w_pb=w_pb, w_pc=w_pc,
             w_o=w_o, g_ffn=g_ffn, w_up=w_up, w_down=w_down, g_ple=g_ple, w_pe=w_pe, w_pg=w_pg)
    depth = w_in.shape[0]
    bp, tp, _ = x_prompt.shape
    bs, ts, _ = x_sample.shape
    n_pages = page_table.shape[1]
    past = n_pages * PAGE
    np_, ns = bp * tp, bs * ts
    tabs_p = _rope_tables(jnp.arange(tp))
    tm_s = min(256, ns)
    tabs_s = _rope_tables(past + (jnp.arange(tm_s) % ts))
    pt_flat = page_table.reshape(-1)
    xp, xs = x_prompt.reshape(np_, D_MODEL), x_sample.reshape(ns, D_MODEL)
    caches = (cache_ckv, jnp.swapaxes(cache_krope, 2, 3), jnp.transpose(cache_k, (0, 1, 3, 4, 2)),
              jnp.transpose(cache_v, (0, 1, 3, 4, 2)), jnp.swapaxes(cache_kidx, 2, 3))
    new_p, new_s = [], []
    for l in range(depth):
        lw = _prep_layer(w, l)
        xp, xs, rows_p, rows_s = _layer(
            xp, xs, p_prompt[l].reshape(np_, D_PLE), p_sample[l].reshape(ns, D_PLE), lw, l,
            caches, state_gla[l], pt_flat, tabs_p, tabs_s,
            (bp, tp, bs, ts, n_pages))
        new_p.append(rows_p)
        new_s.append(rows_s)
    stack = lambda rows, i: jnp.stack([r[i] for r in rows], axis=0)
    return ((xp.reshape(bp, tp, D_MODEL), xs.reshape(bs, ts, D_MODEL))
            + tuple(stack(new_p, i) for i in range(6)) + tuple(stack(new_s, i) for i in range(6)))
```

```python
import functools

import numpy as np
import jax
import jax.numpy as jnp
from jax import lax
from jax.experimental import pallas as pl
from jax.experimental.pallas import tpu as pltpu

D_MODEL = 1024
PAGE = 128
H_A, DN_A, DR_A, DV_A, R_Q, R_KV = 8, 64, 32, 64, 256, 128
SCALE_A = (DN_A + DR_A) ** -0.5
H_B, KVH_B, HD_B = 8, 2, 64
ROT_B = HD_B // 4
HI_B, DI_B = 4, 64
TOPK_MAX = 256
H_C, DK_C, DV_C, R_GATE_C = 4, 64, 128, 16
GLA_TAU = 16.0
GLA_BLOCK = 64
DECODE_PAGES = 32
ONES_ROWS = 16
V_ROWS = HD_B + ONES_ROWS
C_ROWS = R_KV + ONES_ROWS
QHEAD_ORDER = (0, 4, 1, 5, 2, 6, 3, 7)
D_FF = 4 * D_MODEL
D_PLE = 256
ROPE_THETA = 500000.0
EPS = 1e-6

N_A = R_Q + R_KV + DR_A
N_B = H_B * HD_B + 2 * KVH_B * HD_B + HI_B * DI_B + DI_B + HI_B
N_C = 2 * H_C * DK_C + 2 * H_C * DV_C + R_GATE_C
W_A, W_B, W_C = 512, 1152, 1664
LANE = 128
NEG = -1e30
INT_MIN = -(2 ** 31)

F32 = jnp.float32
BF16 = jnp.bfloat16
VMEM_LIMIT = 48 * 1024 * 1024


def _dot(a, b):
    return jnp.dot(a, b, preferred_element_type=F32)


def _dot_nt(a, b):
    return lax.dot_general(a, b, (((1,), (1,)), ((), ())), preferred_element_type=F32)


def _dot_tn(a, b):
    return lax.dot_general(a, b, (((0,), (0,)), ((), ())), preferred_element_type=F32)


def _split3(a):
    a1 = a.astype(BF16)
    r1 = a - a1.astype(F32)
    a2 = r1.astype(BF16)
    a3 = (r1 - a2.astype(F32)).astype(BF16)
    return a1, a2, a3


def _dot_f32(a, b_bf):
    a1, a2, a3 = _split3(a)
    return _dot(a1, b_bf) + _dot(a2, b_bf) + _dot(a3, b_bf)


def _rms(x, g):
    return x * lax.rsqrt(jnp.mean(x * x, axis=-1, keepdims=True) + EPS) * g


def _seg_inv(x, e, et, n):
    inv = lax.rsqrt(_dot_f32(x * x, e) * (1.0 / n) + EPS)
    return _dot_f32(inv, et)


def _rope(x, cos, sin, period, half):
    w = x.shape[-1]
    reps = w // LANE
    if reps > 1:
        cos = jnp.concatenate([cos] * reps, axis=1)
        sin = jnp.concatenate([sin] * reps, axis=1)
    lane = lax.broadcasted_iota(jnp.int32, x.shape, 1)
    first = (lane % period) < half
    partner = jnp.where(first, pltpu.roll(x, w - half, 1), pltpu.roll(x, half, 1))
    return x * cos + partner * sin


def _sigmoid(x):
    return 1.0 / (1.0 + jnp.exp(-x))


def _front_ab_kernel(x_ref, g_ref, wa_ref, wb_ref, gqa_ref, wq_ref, gn_ref, gr_ref, gkv_ref, wuk_ref,
                     wabs_ref, gqb_ref, gkb_ref, ca_ref, sa_ref, cb_ref, sb_ref,
                     e64_ref, e64t_ref, e128_ref, e128t_ref, e64x_ref,
                     q_ref, kcat_ref, c_ref, kr_ref, kinv_ref, ct_ref,
                     bq_ref, bk_ref, bkb_ref, bv_ref, bvt_ref, iq_ref, ik_ref, ikb_ref, iwt_ref):
    xn = _rms(x_ref[...], g_ref[...]).astype(BF16)
    za = _dot_nt(xn, wa_ref[...])
    zb = _dot_nt(xn, wb_ref[...])
    ca, sa, cb, sb = ca_ref[...], sa_ref[...], cb_ref[...], sb_ref[...]
    e64, e64t = e64_ref[...], e64t_ref[...]

    aq = _rms(za[:, :R_Q], gqa_ref[...]).astype(BF16)
    q2 = _dot(aq, wq_ref[...])
    qn, qr = q2[:, :512], q2[:, 512:]
    ss = _dot_f32(qn * qn, e64) + _dot_f32(qr * qr, e128_ref[...])
    inv = lax.rsqrt(ss * (1.0 / (DN_A + DR_A)) + EPS)
    qn = qn * _dot_f32(inv, e64t) * gn_ref[...]
    qr = qr * _dot_f32(inv, e128t_ref[...]) * gr_ref[...]
    qr = _rope(qr, ca, sa, LANE, DR_A // 2) * SCALE_A
    qlat = _dot(qn.astype(BF16), wabs_ref[...])
    for h in range(H_A):
        q_ref[:, h * 256:h * 256 + 128] = qlat[:, h * 128:(h + 1) * 128].astype(BF16)
        q_ref[:, h * 256 + 128:(h + 1) * 256] = qr[:, h * 128:(h + 1) * 128].astype(BF16)

    c = _rms(za[:, R_Q:R_Q + R_KV], gkv_ref[...])
    kr = _rope(za[:, R_Q + R_KV:], ca, sa, LANE, DR_A // 2)
    c_ref[...] = c
    kr_ref[...] = kr[:, :DR_A]
    cb16 = c.astype(BF16)
    kcat_ref[:, :R_KV] = cb16
    kcat_ref[:, R_KV:] = kr.astype(BF16)
    ones = jnp.ones((ONES_ROWS, c.shape[0]), F32)
    ct_ref[...] = jnp.concatenate([c.T, ones], axis=0).astype(BF16)
    kn = _dot(cb16, wuk_ref[...])
    ms = _dot_f32(kn * kn, e64x_ref[...]) + jnp.sum(kr * kr, axis=-1, keepdims=True)
    kinv_ref[...] = lax.rsqrt(ms * (1.0 / (DN_A + DR_A)) + EPS)

    bq = zb[:, :512]
    bq = bq * _seg_inv(bq, e64, e64t, HD_B) * gqb_ref[...]
    bq_ref[...] = (_rope(bq, cb, sb, HD_B, ROT_B // 2) * (HD_B ** -0.5)).astype(BF16)
    bk = zb[:, 512:640]
    bk = bk * _seg_inv(bk, e64[:128], e64t[:, :128], HD_B) * gkb_ref[...]
    bk = _rope(bk, cb, sb, HD_B, ROT_B // 2)
    bk_ref[...] = bk
    bkb_ref[...] = bk.astype(BF16)
    bv = zb[:, 640:768]
    bv_ref[...] = bv
    bvt = bv.T
    bvt_ref[...] = jnp.concatenate([bvt[:HD_B], ones, bvt[HD_B:], ones], axis=0).astype(BF16)
    iq_ref[...] = (_rope(zb[:, 768:1024], cb, sb, DI_B, ROT_B // 2) * (DI_B ** -0.5)).astype(BF16)
    last = zb[:, 1024:1152]
    ik = _rope(last, cb, sb, DI_B, ROT_B // 2)[:, :DI_B]
    ik_ref[...] = ik
    ikb_ref[...] = ik.astype(BF16)
    iwt_ref[...] = (last * (HI_B ** -0.5)).T[DI_B:DI_B + 8, :]


def _front_ab(x2d, tabs, lw, tm):
    n = x2d.shape[0]
    nblk = tabs[0].shape[0] // tm
    row = lambda w: pl.BlockSpec((tm, w), lambda i: (i, 0))
    full = lambda a: pl.BlockSpec(a.shape, lambda i: (0,) * a.ndim)
    tab = pl.BlockSpec((tm, LANE), lambda i: (i % nblk, 0))
    consts = [lw['g_attn'], lw['wa'], lw['wb'], lw['g_qa'], lw['wq2'], lw['gn'], lw['gr'], lw['g_kva'],
              lw['wuk'], lw['wabs'], lw['gqb'], lw['gkb']]
    inds = [lw['e64'], lw['e64t'], lw['e128'], lw['e128t'], lw['e64x']]
    outs = [(2048, BF16, None), (256, BF16, None), (R_KV, F32, None), (DR_A, F32, None),
            (H_A * LANE, F32, None), (C_ROWS, BF16),
            (512, BF16, None), (128, F32, None), (128, BF16, None), (128, F32, None), (KVH_B * V_ROWS, BF16),
            (256, BF16, None), (DI_B, F32, None), (DI_B, BF16, None), (8, F32)]
    out_shape, out_specs = [], []
    for o in outs:
        if len(o) == 2:
            out_shape.append(jax.ShapeDtypeStruct((o[0], n), o[1]))
            out_specs.append(pl.BlockSpec((o[0], tm), lambda i: (0, i)))
        else:
            out_shape.append(jax.ShapeDtypeStruct((n, o[0]), o[1]))
            out_specs.append(row(o[0]))
    return pl.pallas_call(
        _front_ab_kernel, grid=(n // tm,),
        in_specs=[row(D_MODEL)] + [full(a) for a in consts] + [tab] * 4 + [full(a) for a in inds],
        out_specs=out_specs, out_shape=out_shape, name='front_ab',
        compiler_params=pltpu.CompilerParams(dimension_semantics=('arbitrary',), vmem_limit_bytes=VMEM_LIMIT),
    )(x2d, *consts, *tabs, *inds)


def _front_cg_kernel(x_ref, g_ref, wc_ref, wg_ref, wa2_ref, ba_ref,
                     cq_ref, ck_ref, cv_ref, la_ref, cr_ref, gate_ref):
    xn = _rms(x_ref[...], g_ref[...]).astype(BF16)
    zc = _dot_nt(xn, wc_ref[...])
    cq_ref[...] = zc[:, :256]
    ck_ref[...] = zc[:, 256:512] * (DK_C ** -0.5)
    cv_ref[...] = zc[:, 512:1024]
    cr_ref[...] = zc[:, 1024:1536]
    u = _dot(zc[:, 1536:1664].astype(BF16), wa2_ref[...]) + ba_ref[...]
    la_ref[...] = (jnp.minimum(u, 0.0) - jnp.log1p(jnp.exp(-jnp.abs(u)))) * (1.0 / GLA_TAU)
    gate_ref[...] = _dot_nt(xn, wg_ref[...])


def _front_cg(x2d, lw, tm):
    n = x2d.shape[0]
    row = lambda w: pl.BlockSpec((tm, w), lambda i: (i, 0))
    full = lambda a: pl.BlockSpec(a.shape, lambda i: (0,) * a.ndim)
    consts = [lw['g_attn'], lw['wc'], lw['wg'], lw['wa2'], lw['b_a']]
    widths = [256, 256, 512, 256, 512, 3 * D_MODEL]
    return pl.pallas_call(
        _front_cg_kernel, grid=(n // tm,),
        in_specs=[row(D_MODEL)] + [full(a) for a in consts],
        out_specs=[row(w) for w in widths],
        out_shape=[jax.ShapeDtypeStruct((n, w), F32) for w in widths], name='front_cg',
        compiler_params=pltpu.CompilerParams(dimension_semantics=('arbitrary',), vmem_limit_bytes=VMEM_LIMIT),
    )(x2d, *consts)


def _mla_prompt_kernel(q_ref, k_ref, ct_ref, kinv_ref, wuv_ref, o_ref, m_ref, acc_ref, *, tq):
    i, j = pl.program_id(1), pl.program_id(2)

    @pl.when(j == 0)
    def _():
        m_ref[...] = jnp.full(m_ref.shape, NEG, F32)
        acc_ref[...] = jnp.zeros(acc_ref.shape, F32)

    def step(diagonal):
        k = k_ref[...]
        ct = ct_ref[...]
        if diagonal:
            visible = (lax.broadcasted_iota(jnp.int32, (tq, tq), 0) <= lax.broadcasted_iota(jnp.int32, (tq, tq), 1))
        def logits(h):
            kinv = kinv_ref[:, h * LANE:(h + 1) * LANE]
            s = _dot_nt(k, q_ref[:, h * 256:(h + 1) * 256]) * jnp.concatenate([kinv] * (tq // LANE), axis=1)
            return jnp.where(visible, s, NEG) if diagonal else s

        s_next = logits(0)
        for h in range(H_A):
            s = s_next
            if h + 1 < H_A:
                s_next = logits(h + 1)
            m_prev = m_ref[h:h + 1, :]
            m_new = jnp.maximum(m_prev, jnp.max(s, axis=0, keepdims=True))
            p = jnp.exp(s - m_new).astype(BF16)
            acc_ref[h] = jnp.exp(m_prev - m_new) * acc_ref[h] + _dot(ct, p)
            m_ref[h:h + 1, :] = m_new

    @pl.when(j < i)
    def _():
        step(False)

    @pl.when(j == i)
    def _():
        step(True)

    @pl.when(j == pl.num_programs(2) - 1)
    def _():
        olat_t = jnp.concatenate([acc_ref[h, :R_KV, :] / acc_ref[h, R_KV:R_KV + 1, :] for h in range(H_A)],
                                 axis=0)
        o_ref[...] = _dot(olat_t.T.astype(BF16), wuv_ref[...]).astype(BF16)


def _mla_prompt(q, kcat, ct, kinv, wuv_bd, b, t):
    tq = min(256, t)
    nq = t // tq
    kblk = lambda bb, i, j: bb * nq + jnp.minimum(j, i)
    return pl.pallas_call(
        functools.partial(_mla_prompt_kernel, tq=tq), grid=(b, nq, nq),
        in_specs=[pl.BlockSpec((tq, 2048), lambda bb, i, j: (bb * nq + i, 0)),
                  pl.BlockSpec((tq, 256), lambda bb, i, j: (kblk(bb, i, j), 0)),
                  pl.BlockSpec((C_ROWS, tq), lambda bb, i, j: (0, kblk(bb, i, j))),
                  pl.BlockSpec((tq, H_A * LANE), lambda bb, i, j: (kblk(bb, i, j), 0)),
                  pl.BlockSpec(wuv_bd.shape, lambda bb, i, j: (0, 0))],
        out_specs=pl.BlockSpec((tq, 512), lambda bb, i, j: (bb * nq + i, 0)),
        out_shape=jax.ShapeDtypeStruct((b * t, 512), BF16),
        scratch_shapes=[pltpu.VMEM((H_A, tq), F32), pltpu.VMEM((H_A, C_ROWS, tq), F32)], name='mla_prompt',
        compiler_params=pltpu.CompilerParams(dimension_semantics=('arbitrary', 'arbitrary', 'arbitrary'),
                                             vmem_limit_bytes=VMEM_LIMIT),
    )(q, kcat, ct, kinv, wuv_bd)


def _score_key(sc):
    sc = jnp.where(sc == 0.0, 0.0, sc)
    bits = lax.bitcast_convert_type(sc, jnp.int32)
    return jnp.where(bits < 0, bits ^ 0x7FFFFFFF, bits)


def _kth_largest(count_ge, k, shape):
    base = jnp.where(count_ge(jnp.zeros(shape, jnp.int32)) >= k, 0, INT_MIN).astype(jnp.int32)

    def body(it, base):
        cand = base + jnp.left_shift(jnp.int32(1), 30 - it)
        return jnp.where(count_ge(cand) >= k, cand, base)
    return lax.fori_loop(0, 31, body, base)


def _cumsum_lanes(x):
    r, w = x.shape
    nb = w // LANE
    xs = jnp.concatenate([x[:, j * LANE:(j + 1) * LANE] for j in range(nb)], axis=0) if nb > 1 else x
    a = lax.broadcasted_iota(jnp.int32, (LANE, LANE), 0)
    b = lax.broadcasted_iota(jnp.int32, (LANE, LANE), 1)
    p = _dot(xs.astype(BF16), jnp.where(a <= b, 1.0, 0.0).astype(BF16))
    if nb == 1:
        return p
    ra = lax.broadcasted_iota(jnp.int32, (nb * r, nb * r), 0)
    rb = lax.broadcasted_iota(jnp.int32, (nb * r, nb * r), 1)
    earlier = jnp.where((rb // r < ra // r) & (rb % r == ra % r), 1.0, 0.0).astype(BF16)
    tot = jnp.broadcast_to(p[:, LANE - 1:LANE], (nb * r, LANE)).astype(BF16)
    p = p + _dot(earlier, tot)
    return jnp.concatenate([p[j * r:(j + 1) * r, :] for j in range(nb)], axis=1)


def _select(keys, kth, need, carry):
    eq = (keys == kth).astype(F32)
    pref = _cumsum_lanes(eq) - eq + carry
    sel = (keys > kth) | ((eq > 0.0) & (pref < need))
    return sel, carry + jnp.sum(eq, axis=-1, keepdims=True)


def _dsa_prompt_kernel(bq_ref, iq_ref, iwt_ref, ik_ref, k_ref, vt_ref, o_ref, keys_ref, bias_ref,
                       *, tq, tkb, nkb, tile0, n_sel):
    i = pl.program_id(1)
    blocks = [slice(kb * tkb, (kb + 1) * tkb) for kb in range(nkb)]
    keyp = (nkb - 1) * tkb + lax.broadcasted_iota(jnp.int32, (tkb, tq), 0)
    visible = keyp <= (tile0 + i) * tq + lax.broadcasted_iota(jnp.int32, (tkb, tq), 1)
    iwt = iwt_ref[...]
    iq = [iq_ref[:, h * DI_B:(h + 1) * DI_B] for h in range(HI_B)]

    for kb, blk in enumerate(blocks):
        ik = ik_ref[blk, :]
        sc = jnp.zeros((tkb, tq), F32)
        for h in range(HI_B):
            sc = sc + jnp.maximum(_dot_nt(ik, iq[h]), 0.0) * iwt[h:h + 1, :]
        if kb == nkb - 1:
            sc = jnp.where(visible, sc, -jnp.inf)
        keys_ref[blk, :] = _score_key(sc)

    def count(pred):
        acc = jnp.zeros((1, tq), F32)
        for blk in blocks:
            acc = acc + jnp.sum(jnp.where(pred(keys_ref[blk, :]), 1.0, 0.0), axis=0, keepdims=True)
        return acc

    kth = _kth_largest(lambda cand: count(lambda x: x >= cand), float(n_sel), (1, tq))
    need = float(n_sel) - count(lambda x: x > kth)

    lower = jnp.where(lax.broadcasted_iota(jnp.int32, (tkb, tkb), 1) < lax.broadcasted_iota(jnp.int32, (tkb, tkb), 0),
                      1.0, 0.0).astype(BF16)
    ties = jnp.zeros((1, tq), F32)
    for kb, blk in enumerate(blocks):
        keys = keys_ref[blk, :]
        eq = keys == kth
        eqf = jnp.where(eq, 1.0, 0.0)
        before = _dot(lower, eqf.astype(BF16)) + ties
        sel = (keys > kth) | (eq & (before < need))
        if kb == nkb - 1:
            sel = sel & visible
        bias_ref[blk, :] = jnp.where(sel, 0.0, NEG)
        ties = ties + jnp.sum(eqf, axis=0, keepdims=True)

    upper_half = lax.broadcasted_iota(jnp.int32, (tq, LANE), 1) >= HD_B

    def logits(pos):
        qb = bq_ref[:, (pos // 2) * LANE:(pos // 2 + 1) * LANE]
        qpad = jnp.where(upper_half == (pos % 2 == 1), qb, jnp.zeros_like(qb))
        return [_dot_nt(k_ref[blk, :], qpad) + bias_ref[blk, :] for blk in blocks]

    outs = [None] * H_B
    s_next = logits(0)
    for pos in range(H_B):
        s = s_next
        if pos + 1 < H_B:
            s_next = logits(pos + 1)
        n = pos % 2
        m = functools.reduce(jnp.maximum, [jnp.max(sb, axis=0, keepdims=True) for sb in s])
        acc = jnp.zeros((V_ROWS, tq), F32)
        for sb, blk in zip(s, blocks):
            acc = acc + _dot(vt_ref[n * V_ROWS:(n + 1) * V_ROWS, blk], jnp.exp(sb - m).astype(BF16))
        outs[QHEAD_ORDER[pos]] = acc[:HD_B, :] / acc[HD_B:HD_B + 1, :]
    o_ref[0] = jnp.concatenate(outs, axis=0).T.astype(BF16)


def _dsa_prompt(bq, iq, iwt, ikb, bkb, bvt, b, t):
    tq = min(128, t)
    tkb = min(512, t)
    nq = t // tq
    tpc = tkb // tq
    n_sel = min(TOPK_MAX, t // 4)
    seq = lambda w: pl.BlockSpec((t, w), lambda bb, i: (bb, 0))
    outs = []
    for cls in range(t // tkb):
        nkb = cls + 1
        qrow = lambda w, cls=cls: pl.BlockSpec((tq, w), lambda bb, i: (bb * nq + cls * tpc + i, 0))
        outs.append(pl.pallas_call(
            functools.partial(_dsa_prompt_kernel, tq=tq, tkb=tkb, nkb=nkb, tile0=cls * tpc, n_sel=n_sel),
            grid=(b, tpc),
            in_specs=[qrow(512), qrow(256),
                      pl.BlockSpec((8, tq), lambda bb, i, cls=cls: (0, bb * nq + cls * tpc + i)),
                      seq(DI_B), seq(128), pl.BlockSpec((KVH_B * V_ROWS, t), lambda bb, i: (0, bb))],
            out_specs=pl.BlockSpec((1, tq, 512), lambda bb, i: (bb, i, 0)),
            out_shape=jax.ShapeDtypeStruct((b, tkb, 512), BF16),
            scratch_shapes=[pltpu.VMEM((nkb * tkb, tq), jnp.int32), pltpu.VMEM((nkb * tkb, tq), F32)],
            name='dsa_prompt',
            compiler_params=pltpu.CompilerParams(dimension_semantics=('arbitrary', 'arbitrary'),
                                                 vmem_limit_bytes=VMEM_LIMIT),
        )(bq, iq, iwt, ikb, bkb, bvt))
    return jnp.concatenate(outs, axis=1).reshape(b * t, 512)


def _gla_kernel(q_ref, k_ref, v_ref, la_ref, s0_ref, o_ref, sout_ref, st_ref, *, c, nsub):
    j = pl.program_id(1)

    @pl.when(j == 0)
    def _():
        st_ref[...] = s0_ref[0]

    kw = H_C * DK_C
    r = lax.broadcasted_iota(jnp.int32, (c, c), 0)
    cc = lax.broadcasted_iota(jnp.int32, (c, c), 1)
    tril = jnp.where(r >= cc, 1.0, 0.0).astype(BF16)
    row = lax.broadcasted_iota(jnp.int32, (c, kw), 0)
    head = lax.broadcasted_iota(jnp.int32, (c, kw), 1) // DK_C

    def heads(a):
        return [jnp.where(head == h, a, 0.0).astype(BF16) for h in range(H_C)]

    def body(i, carry):
        off = pl.multiple_of(i * c, c)
        la = la_ref[0, pl.ds(off, c), :]
        l1, l2, l3 = _split3(la)
        lb = _dot(tril, l1) + _dot(tril, l2) + _dot(tril, l3)
        q = q_ref[0, pl.ds(off, c), :]
        k = k_ref[0, pl.ds(off, c), :]
        v = v_ref[0, pl.ds(off, c), :]

        k16 = k.astype(BF16)
        att = [jnp.where(r == cc, _dot_nt(qh, k16), 0.0) for qh in heads(q)]
        fl = lb
        b = 1
        while b < c:
            up = (row % (2 * b)) >= b
            if b > 1:
                fl = jnp.where((row % b) >= b // 2, pltpu.roll(fl, b // 2, 0), fl)
            qf = jnp.where(up, q * jnp.exp(jnp.where(up, lb - fl, 0.0)), 0.0)
            nxt = pltpu.roll(fl, c - b, 0)
            kf = jnp.where(up, 0.0, k * jnp.exp(jnp.where(up, 0.0, nxt - lb))).astype(BF16)
            same = (r // (2 * b)) == (cc // (2 * b))
            att = [a + jnp.where(same, _dot_nt(qh, kf), 0.0) for a, qh in zip(att, heads(qf))]
            b *= 2

        st = st_ref[...]
        st16 = st.astype(BF16)
        lb_end = lb[c - 1:c, :]
        upd = jnp.zeros(st.shape, F32)
        for h, (qe, kt) in enumerate(zip(heads(q * jnp.exp(lb)), heads(k * jnp.exp(lb_end - lb)))):
            vh = v[:, h * DV_C:(h + 1) * DV_C].astype(BF16)
            o_ref[0, pl.ds(off, c), h * DV_C:(h + 1) * DV_C] = _dot(att[h].astype(BF16), vh) + _dot_nt(qe, st16)
            upd = upd + _dot_tn(vh, kt)
        st_ref[...] = st * jnp.exp(lb_end) + upd
        return carry
    lax.fori_loop(0, nsub, body, 0)

    @pl.when(j == pl.num_programs(1) - 1)
    def _():
        sout_ref[0] = st_ref[...]


def _gla(cq, ck, cv, la, s0t, b, t):
    c = min(GLA_BLOCK, t)
    tt = min(512, t)
    r3 = lambda a: a.reshape(b, t, a.shape[-1])
    tok = lambda w: pl.BlockSpec((1, tt, w), lambda bb, j: (bb, j, 0))
    st = pl.BlockSpec((1, DV_C, H_C * DK_C), lambda bb, j: (bb, 0, 0))
    oc, s_out = pl.pallas_call(
        functools.partial(_gla_kernel, c=c, nsub=tt // c), grid=(b, t // tt),
        in_specs=[tok(256), tok(256), tok(512), tok(256), st],
        out_specs=[tok(512), st],
        out_shape=[jax.ShapeDtypeStruct((b, t, 512), F32), jax.ShapeDtypeStruct((b, DV_C, H_C * DK_C), F32)],
        scratch_shapes=[pltpu.VMEM((DV_C, H_C * DK_C), F32)], name='gla',
        compiler_params=pltpu.CompilerParams(dimension_semantics=('arbitrary', 'arbitrary'),
                                             vmem_limit_bytes=VMEM_LIMIT),
    )(r3(cq), r3(ck), r3(cv), r3(la), s0t)
    return oc.reshape(b * t, 512), s_out


def _back_merge_kernel(x_ref, oa_ref, ob_ref, oc_ref, cr_ref, gate_ref, goc_ref,
                       wpa_ref, wpb_ref, wpc_ref, wo_ref, y_ref):
    oc, cr = oc_ref[...], cr_ref[...]
    parts = []
    for h in range(H_C):
        och = oc[:, h * DV_C:(h + 1) * DV_C]
        parts.append(och * lax.rsqrt(jnp.mean(och * och, axis=-1, keepdims=True) + EPS))
    ocn = jnp.concatenate(parts, axis=1) * goc_ref[...] * (cr * _sigmoid(cr))
    gate = gate_ref[...]
    merged = (_sigmoid(gate[:, :D_MODEL]) * _dot(oa_ref[...], wpa_ref[...])
              + _sigmoid(gate[:, D_MODEL:2 * D_MODEL]) * _dot(ob_ref[...], wpb_ref[...])
              + _sigmoid(gate[:, 2 * D_MODEL:]) * _dot(ocn.astype(BF16), wpc_ref[...]))
    y_ref[...] = x_ref[...] + _dot(merged.astype(BF16), wo_ref[...])


def _back_merge(x2d, oa, ob, oc, cr, gates, lw, tm):
    n = x2d.shape[0]
    row = lambda w: pl.BlockSpec((tm, w), lambda i: (i, 0))
    full = lambda a: pl.BlockSpec(a.shape, lambda i: (0,) * a.ndim)
    consts = [lw['goc'], lw['w_pa'], lw['w_pb'], lw['w_pc'], lw['w_o']]
    return pl.pallas_call(
        _back_merge_kernel, grid=(n // tm,),
        in_specs=[row(D_MODEL), row(512), row(512), row(512), row(512), row(3 * D_MODEL)]
        + [full(a) for a in consts],
        out_specs=row(D_MODEL), out_shape=jax.ShapeDtypeStruct((n, D_MODEL), F32), name='back_merge',
        compiler_params=pltpu.CompilerParams(dimension_semantics=('arbitrary',), vmem_limit_bytes=VMEM_LIMIT),
    )(x2d, oa, ob, oc, cr, gates, *consts)


def _back_ffn_kernel(x_ref, p_ref, gf_ref, wup_ref, wdn_ref, gp_ref, wpg_ref, wpe_ref, y_ref, xn_ref, acc_ref):
    f = pl.program_id(1)

    @pl.when(f == 0)
    def _():
        xn_ref[...] = _rms(x_ref[...], gf_ref[...]).astype(BF16)
        acc_ref[...] = x_ref[...]

    hid = jnp.maximum(_dot(xn_ref[...], wup_ref[...]), 0.0)
    acc_ref[...] += _dot((hid * hid).astype(BF16), wdn_ref[...])

    @pl.when(f == pl.num_programs(1) - 1)
    def _():
        x2 = acc_ref[...]
        gate = _sigmoid(_dot(_rms(x2, gp_ref[...]).astype(BF16), wpg_ref[...]))
        y_ref[...] = x2 + gate * _dot(p_ref[...].astype(BF16), wpe_ref[...])


def _back_ffn(x2d, p2d, lw, tm):
    n = x2d.shape[0]
    tf = 1024
    row = lambda w: pl.BlockSpec((tm, w), lambda i, f: (i, 0))
    full = lambda a: pl.BlockSpec(a.shape, lambda i, f: (0,) * a.ndim)
    return pl.pallas_call(
        _back_ffn_kernel, grid=(n // tm, D_FF // tf),
        in_specs=[row(D_MODEL), row(D_PLE), full(lw['g_ffn']),
                  pl.BlockSpec((D_MODEL, tf), lambda i, f: (0, f)), pl.BlockSpec((tf, D_MODEL), lambda i, f: (f, 0)),
                  full(lw['g_ple']), full(lw['w_pg']), full(lw['w_pe'])],
        out_specs=row(D_MODEL), out_shape=jax.ShapeDtypeStruct((n, D_MODEL), F32),
        scratch_shapes=[pltpu.VMEM((tm, D_MODEL), BF16), pltpu.VMEM((tm, D_MODEL), F32)], name='back_ffn',
        compiler_params=pltpu.CompilerParams(dimension_semantics=('arbitrary', 'arbitrary'),
                                             vmem_limit_bytes=VMEM_LIMIT),
    )(x2d, p2d, lw['g_ffn'], lw['w_up'], lw['w_down'], lw['g_ple'], lw['w_pg'], lw['w_pe'])


def _page_copies(pt_ref, first, srcs, bufs, sems, slot, npg):
    copies = []
    for pg in range(npg):
        page = pt_ref[first + pg]
        for c, (src, buf) in enumerate(zip(srcs, bufs)):
            copies.append(pltpu.make_async_copy(src.at[page], buf.at[slot, pg], sems.at[slot, c]))
    return copies


def _fetch_pages(pt_ref, srcs, bufs, sems, npg):
    n = pl.program_id(0) * pl.num_programs(1) + pl.program_id(1)
    total = pl.num_programs(0) * pl.num_programs(1)
    slot = n % 2

    @pl.when(n == 0)
    def _():
        for cp in _page_copies(pt_ref, 0, srcs, bufs, sems, 0, npg):
            cp.start()

    for cp in _page_copies(pt_ref, n * npg, srcs, bufs, sems, slot, npg):
        cp.wait()

    @pl.when(n + 1 < total)
    def _():
        for cp in _page_copies(pt_ref, (n + 1) * npg, srcs, bufs, sems, 1 - slot, npg):
            cp.start()
    return slot


def _mla_decode_kernel(pt_ref, q_ref, iq_ref, iw_ref, cn_ref, krn_ref, kinvn_ref, ikn_ref,
                       wukt_ref, wuv_ref, ckv_hbm, krp_hbm, kix_hbm,
                       o_ref, keys_ref, keysn_ref, m_ref, l_ref, acc_ref, s_ref, c16_ref,
                       ckv_buf, krp_buf, kix_buf, sems, *, npg, cpg, t):
    slot = _fetch_pages(pt_ref, (ckv_hbm, krp_hbm, kix_hbm), (ckv_buf, krp_buf, kix_buf), sems, npg)
    ckv = [ckv_buf.at[slot, pg] for pg in range(npg)]
    krp = [krp_buf.at[slot, pg] for pg in range(npg)]
    kix = [kix_buf.at[slot, pg] for pg in range(npg)]
    g = pl.program_id(1)
    rows = H_A * t
    nup = H_A * DN_A

    @pl.when(g == 0)
    def _():
        m_ref[...] = jnp.full(m_ref.shape, NEG, F32)
        l_ref[...] = jnp.zeros(l_ref.shape, F32)
        acc_ref[...] = jnp.zeros(acc_ref.shape, F32)

    q = q_ref[0]
    qlat, qr = q[:, :R_KV], q[:, R_KV:R_KV + DR_A]
    lhs = jnp.concatenate([wukt_ref[...], qlat], axis=0)
    iq = iq_ref[0]
    iw = iw_ref[0]

    def softmax_step(s, v16):
        m_prev = m_ref[...]
        m_new = jnp.maximum(m_prev, jnp.max(s, axis=-1, keepdims=True))
        p = jnp.exp(s - m_new)
        alpha = jnp.exp(m_prev - m_new)
        l_ref[...] = alpha * l_ref[...] + jnp.sum(p, axis=-1, keepdims=True)
        acc_ref[...] = alpha * acc_ref[...] + _dot(p.astype(BF16), v16)
        m_ref[...] = m_new

    def per_head(s, kinv_rows):
        return jnp.concatenate([s[h * t:(h + 1) * t, :] * kinv_rows[h] for h in range(H_A)], axis=0)

    def combine_heads(d):
        sc = jnp.zeros((t, d.shape[1]), F32)
        for h in range(HI_B):
            sc = sc + d[h * t:(h + 1) * t, :] * iw[h * t:(h + 1) * t, :1]
        return sc

    nch = npg // cpg
    chunk_cols = [slice(ch * cpg * PAGE, (ch + 1) * cpg * PAGE) for ch in range(nch)]

    def products(ch):
        pages = range(ch * cpg, (ch + 1) * cpg)
        c16 = jnp.concatenate([ckv[pg][...].astype(BF16) for pg in pages], axis=0)
        c16_ref[chunk_cols[ch], :] = c16
        krt = jnp.concatenate([krp[pg][...] for pg in pages], axis=1)
        ikt = jnp.concatenate([kix[pg][...].astype(BF16) for pg in pages], axis=1)
        r = _dot_nt(lhs, c16)
        return r, _dot(qr, krt.astype(BF16)), jnp.sum(krt * krt, axis=0, keepdims=True), _dot(iq, ikt)

    nxt = products(0)
    for ch in range(nch):
        r, s_rope, kr2, d = nxt
        if ch + 1 < nch:
            nxt = products(ch + 1)
        kinv = []
        for h in range(H_A):
            knh = r[h * DN_A:(h + 1) * DN_A, :]
            ms = jnp.sum(knh * knh, axis=0, keepdims=True) + kr2
            kinv.append(lax.rsqrt(ms * (1.0 / (DN_A + DR_A)) + EPS))
        s_ref[:, chunk_cols[ch]] = per_head(r[nup:, :] + s_rope, kinv)
        keys_ref[0, :, chunk_cols[ch]] = _score_key(combine_heads(jnp.maximum(d, 0.0)))
    softmax_step(s_ref[...], c16_ref[...])

    @pl.when(g == pl.num_programs(1) - 1)
    def _():
        pad = lambda a: jnp.concatenate([a, jnp.zeros((PAGE - t, a.shape[1]), a.dtype)], axis=0)
        col = lax.broadcasted_iota(jnp.int32, (rows, PAGE), 1)
        qt = lax.broadcasted_iota(jnp.int32, (rows, PAGE), 0) % t
        cn16 = pad(cn_ref[0]).astype(BF16)
        kinvn = kinvn_ref[0]
        s = _dot_nt(qlat, cn16) + _dot_nt(qr, pad(krn_ref[0]).astype(BF16))
        s = per_head(s, [kinvn[h:h + 1, :] for h in range(H_A)])
        softmax_step(jnp.where(col <= qt, s, NEG), cn16)
        dn = jnp.maximum(_dot_nt(iq, pad(ikn_ref[0]).astype(BF16)), 0.0)
        scn = jnp.where((col <= qt)[:t], combine_heads(dn), -jnp.inf)
        keysn_ref[0] = _score_key(scn)
        o = acc_ref[...] / l_ref[...]
        olat = jnp.concatenate([o[h * t:(h + 1) * t, :] for h in range(H_A)], axis=1).astype(BF16)
        o_ref[0] = _dot(olat, wuv_ref[...]).astype(BF16)


def _mla_decode(pt_flat, qs, iqs, iws, cn, krn, kinvn, ikn, lw, cache_ckv, krope_t, kidx_t, b, t, n_pages):
    npg = min(DECODE_PAGES, n_pages)
    cpg = min(4, npg)
    ng = n_pages // npg
    past = n_pages * PAGE
    per_b = lambda a: pl.BlockSpec((1,) + a.shape[1:], lambda bb, g, pt: (bb,) + (0,) * (a.ndim - 1))
    full = lambda a: pl.BlockSpec(a.shape, lambda bb, g, pt: (0,) * a.ndim)

    small = [qs, iqs, iws, cn, krn, kinvn, ikn]
    consts = [lw['wukt'], lw['wuv_bd']]
    in_specs = ([per_b(a) for a in small] + [full(a) for a in consts]
                + [pl.BlockSpec(memory_space=pl.ANY)] * 3)
    grid_spec = pltpu.PrefetchScalarGridSpec(
        num_scalar_prefetch=1, grid=(b, ng), in_specs=in_specs,
        out_specs=[pl.BlockSpec((1, t, 512), lambda bb, g, pt: (bb, 0, 0)),
                   pl.BlockSpec((1, t, npg * PAGE), lambda bb, g, pt: (bb, 0, g)),
                   pl.BlockSpec((1, t, PAGE), lambda bb, g, pt: (bb, 0, 0))],
        scratch_shapes=[pltpu.VMEM((H_A * t, 1), F32), pltpu.VMEM((H_A * t, 1), F32),
                        pltpu.VMEM((H_A * t, R_KV), F32), pltpu.VMEM((H_A * t, npg * PAGE), F32),
                        pltpu.VMEM((npg * PAGE, R_KV), BF16),
                        pltpu.VMEM((2, npg, PAGE, R_KV), F32), pltpu.VMEM((2, npg, DR_A, PAGE), F32),
                        pltpu.VMEM((2, npg, DI_B, PAGE), F32), pltpu.SemaphoreType.DMA((2, 3))])
    return pl.pallas_call(
        functools.partial(_mla_decode_kernel, npg=npg, cpg=cpg, t=t), grid_spec=grid_spec,
        out_shape=[jax.ShapeDtypeStruct((b, t, 512), BF16), jax.ShapeDtypeStruct((b, t, past), jnp.int32),
                   jax.ShapeDtypeStruct((b, t, PAGE), jnp.int32)], name='mla_decode',
        compiler_params=pltpu.CompilerParams(dimension_semantics=('arbitrary', 'arbitrary'),
                                             vmem_limit_bytes=VMEM_LIMIT),
    )(pt_flat, *small, *consts, cache_ckv, krope_t, kidx_t)


def _threshold_kernel(keys_ref, keysn_ref, kth_ref, need_ref, tie_ref, *, n_sel):
    nb, t, past = keys_ref.shape
    keys = keys_ref[...].reshape(nb * t, past)
    keysn = keysn_ref[...].reshape(nb * t, PAGE)

    def count(pred):
        return (jnp.sum(jnp.where(pred(keys), 1.0, 0.0), axis=-1, keepdims=True)
                + jnp.sum(jnp.where(pred(keysn), 1.0, 0.0), axis=-1, keepdims=True))
    kth = _kth_largest(lambda cand: count(lambda x: x >= cand), float(n_sel), (nb * t, 1))
    need = float(n_sel) - count(lambda x: x > kth)
    tied = count(lambda x: x == kth)
    wide = lambda a: jnp.broadcast_to(a, (nb * t, LANE)).reshape(nb, t, LANE)
    kth_ref[...] = wide(kth)
    need_ref[...] = wide(need)
    tie_ref[...] = wide(jnp.where(tied > need, 1.0, 0.0))


def _threshold(keys, keysn, n_sel):
    b, t, past = keys.shape
    nb = min(8, b)
    blk = lambda w: pl.BlockSpec((nb, t, w), lambda i: (i, 0, 0))
    return pl.pallas_call(
        functools.partial(_threshold_kernel, n_sel=n_sel), grid=(b // nb,),
        in_specs=[blk(past), blk(PAGE)], out_specs=[blk(LANE)] * 3,
        out_shape=[jax.ShapeDtypeStruct((b, t, LANE), jnp.int32), jax.ShapeDtypeStruct((b, t, LANE), F32),
                   jax.ShapeDtypeStruct((b, t, LANE), F32)], name='topk_threshold',
        compiler_params=pltpu.CompilerParams(dimension_semantics=('arbitrary',), vmem_limit_bytes=VMEM_LIMIT),
    )(keys, keysn)


def _dsa_decode_kernel(pt_ref, tie_ref, keys_ref, keysn_ref, kth_ref, need_ref, bq_ref, kn_ref, vn_ref,
                       k_hbm, v_hbm, o_ref, carry_ref, bias_ref, m_ref, l_ref, acc_ref, k_buf, v_buf, sems,
                       *, npg, t):
    slot = _fetch_pages(pt_ref, (k_hbm, v_hbm), (k_buf, v_buf), sems, npg)
    kp = [k_buf.at[slot, pg] for pg in range(npg)]
    vp = [v_buf.at[slot, pg] for pg in range(npg)]
    bb, g = pl.program_id(0), pl.program_id(1)
    nk = npg * PAGE
    gsz = H_B // KVH_B

    @pl.when(g == 0)
    def _():
        carry_ref[...] = jnp.zeros(carry_ref.shape, F32)
        m_ref[...] = jnp.full(m_ref.shape, NEG, F32)
        l_ref[...] = jnp.zeros(l_ref.shape, F32)
        acc_ref[...] = jnp.zeros(acc_ref.shape, F32)

    q = bq_ref[0]
    kth = kth_ref[0][:, :1]
    need = need_ref[0][:, :1]
    ties_matter = tie_ref[bb] != 0

    def attend(keys_blk, extra, scores, values):
        w = keys_blk.shape[1]

        visible = True if extra is None else extra

        @pl.when(ties_matter)
        def _():
            sel, carry = _select(keys_blk, kth, need, carry_ref[...])
            carry_ref[...] = carry
            bias_ref[:, :w] = jnp.where(sel & visible, 0.0, NEG)

        @pl.when(jnp.logical_not(ties_matter))
        def _():
            bias_ref[:, :w] = jnp.where((keys_blk >= kth) & visible, 0.0, NEG)

        rs = [slice(n * gsz * t, (n + 1) * gsz * t) for n in range(KVH_B)]
        s = jnp.concatenate([scores(n, q[rs[n], :]) for n in range(KVH_B)], axis=0)
        s = s + jnp.concatenate([bias_ref[:, :w]] * H_B, axis=0)
        m_prev = m_ref[...]
        m_new = jnp.maximum(m_prev, jnp.max(s, axis=-1, keepdims=True))
        p = jnp.exp(s - m_new)
        alpha = jnp.exp(m_prev - m_new)
        l_ref[...] = alpha * l_ref[...] + jnp.sum(p, axis=-1, keepdims=True)
        pv = jnp.concatenate([values(n, p[rs[n], :].astype(BF16)) for n in range(KVH_B)], axis=0)
        acc_ref[...] = alpha * acc_ref[...] + pv
        m_ref[...] = m_new

    kt = [jnp.concatenate([kp[pg][n].astype(BF16) for pg in range(npg)], axis=1) for n in range(KVH_B)]
    vt = [jnp.concatenate([vp[pg][n].astype(BF16) for pg in range(npg)], axis=1) for n in range(KVH_B)]
    off = pl.multiple_of(g * nk, nk)
    attend(keys_ref[0, :, pl.ds(off, nk)], None,
           lambda n, qn: _dot(qn, kt[n]), lambda n, p16: _dot_nt(p16, vt[n]))

    @pl.when(g == pl.num_programs(1) - 1)
    def _():
        pad = lambda a: jnp.concatenate([a, jnp.zeros((PAGE - t, a.shape[1]), a.dtype)], axis=0)
        col = lax.broadcasted_iota(jnp.int32, (t, PAGE), 1)
        qt = lax.broadcasted_iota(jnp.int32, (t, PAGE), 0)
        k16, v16 = pad(kn_ref[0]).astype(BF16), pad(vn_ref[0]).astype(BF16)
        attend(keysn_ref[0], col <= qt,
               lambda n, qn: _dot_nt(qn, k16[:, n * HD_B:(n + 1) * HD_B]),
               lambda n, p16: _dot(p16, v16[:, n * HD_B:(n + 1) * HD_B]))
        o = acc_ref[...] / l_ref[...]
        o_ref[0] = jnp.concatenate([o[h * t:(h + 1) * t, :] for h in range(H_B)], axis=1).astype(BF16)


def _dsa_decode(pt_flat, keys, keysn, bqs, kn, vn, k_t, v_t, b, t, n_pages):
    npg = min(DECODE_PAGES, n_pages)
    ng = n_pages // npg
    past = n_pages * PAGE
    kth, need, tie = _threshold(keys, keysn, min(TOPK_MAX, (past + t) // 4))
    tie_flag = (jnp.max(tie, axis=(1, 2)) > 0.0).astype(jnp.int32)
    per_b = lambda a: pl.BlockSpec((1,) + a.shape[1:], lambda bb, g, pt, tf: (bb,) + (0,) * (a.ndim - 1))

    small = [keys, keysn, kth, need, bqs, kn, vn]
    grid_spec = pltpu.PrefetchScalarGridSpec(
        num_scalar_prefetch=2, grid=(b, ng),
        in_specs=[per_b(a) for a in small] + [pl.BlockSpec(memory_space=pl.ANY)] * 2,
        out_specs=pl.BlockSpec((1, t, 512), lambda bb, g, pt, tf: (bb, 0, 0)),
        scratch_shapes=[pltpu.VMEM((t, 1), F32), pltpu.VMEM((t, npg * PAGE), F32),
                        pltpu.VMEM((H_B * t, 1), F32), pltpu.VMEM((H_B * t, 1), F32),
                        pltpu.VMEM((H_B * t, HD_B), F32),
                        pltpu.VMEM((2, npg, KVH_B, HD_B, PAGE), F32), pltpu.VMEM((2, npg, KVH_B, HD_B, PAGE), F32),
                        pltpu.SemaphoreType.DMA((2, 2))])
    return pl.pallas_call(
        functools.partial(_dsa_decode_kernel, npg=npg, t=t), grid_spec=grid_spec,
        out_shape=jax.ShapeDtypeStruct((b, t, 512), BF16), name='dsa_decode',
        compiler_params=pltpu.CompilerParams(dimension_semantics=('arbitrary', 'arbitrary'),
                                             vmem_limit_bytes=VMEM_LIMIT),
    )(pt_flat, tie_flag, *small, k_t, v_t)


def _indicator(width, seg):
    e = np.zeros((width, LANE), np.float32)
    e[np.arange(width), np.arange(width) // seg] = 1.0
    return e


def _prep_layer(w, l):
    w_in_t = jnp.swapaxes(w['w_in'], 1, 2)[l]
    o_b, o_c, o_g = N_A, N_A + N_B, N_A + N_B + N_C
    w_uq = w['w_uq'][l].reshape(R_Q, H_A, DN_A + DR_A)
    wq_r = jnp.pad(w_uq[:, :, DN_A:], ((0, 0), (0, 0), (0, LANE - DR_A))).reshape(R_Q, H_A * LANE)
    g_q_a = w['g_q_a'][l]
    w_uk = w['w_uk'][l]
    eye = jnp.eye(H_A, dtype=F32)
    wabs = (eye[:, None, :, None] * jnp.transpose(w_uk, (1, 2, 0))[:, :, None, :]).reshape(H_A * DN_A, H_A * R_KV)
    wuv_bd = (eye[:, None, :, None] * jnp.transpose(w['w_uv'][l], (1, 0, 2))[:, :, None, :]
              ).reshape(H_A * R_KV, H_A * DV_A)
    e64 = _indicator(512, 64)
    e128 = _indicator(1024, 128)
    row = lambda a: a.reshape(1, -1)
    nq_b = H_B * HD_B
    padr = lambda a, n: jnp.pad(a, ((0, n - a.shape[0]), (0, 0))).astype(BF16)
    wbq = w_in_t[o_b:o_b + nq_b].reshape(H_B, HD_B, D_MODEL)[np.array(QHEAD_ORDER)].reshape(nq_b, D_MODEL)
    wb = jnp.concatenate([wbq, w_in_t[o_b + nq_b:o_c]], axis=0)
    return dict(
        g_attn=row(w['g_attn'][l]),
        wa=padr(w_in_t[:o_b], W_A), wb=padr(wb, W_B), wc=padr(w_in_t[o_c:o_g], W_C), wg=w_in_t[o_g:].astype(BF16),
        g_qa=row(w['g_qa'][l]),
        wq2=jnp.concatenate([w_uq[:, :, :DN_A].reshape(R_Q, H_A * DN_A), wq_r], axis=1).astype(BF16),
        gn=row(jnp.tile(g_q_a[:DN_A], H_A)),
        gr=row(jnp.tile(jnp.pad(g_q_a[DN_A:], (0, LANE - DR_A)), H_A)),
        g_kva=row(w['g_kva'][l]),
        wuk=w_uk.reshape(R_KV, H_A * DN_A).astype(BF16), wukt=w_uk.reshape(R_KV, H_A * DN_A).T.astype(BF16),
        wabs=(wabs * SCALE_A).astype(BF16), wuv_bd=wuv_bd.astype(BF16),
        gqb=row(jnp.tile(w['g_q_b'][l], H_B)), gkb=row(jnp.tile(w['g_k_b'][l], KVH_B)),
        e64=jnp.asarray(e64, BF16), e64t=jnp.asarray(e64.T, BF16),
        e128=jnp.asarray(e128, BF16), e128t=jnp.asarray(e128.T, BF16),
        e64x=jnp.asarray(np.kron(np.eye(H_A, dtype=np.float32), np.ones((DN_A, LANE), np.float32)), BF16),
        wa2=jnp.pad(w['w_a2'][l], ((0, LANE - R_GATE_C), (0, 0))).astype(BF16), b_a=row(w['b_a'][l]),
        goc=row(jnp.tile(w['g_o_c'][l], H_C)),
        w_pa=w['w_pa'][l].astype(BF16), w_pb=w['w_pb'][l].astype(BF16), w_pc=w['w_pc'][l].astype(BF16),
        w_o=w['w_o'][l].astype(BF16), g_ffn=row(w['g_ffn'][l]),
        w_up=w['w_up'][l].astype(BF16), w_down=w['w_down'][l].astype(BF16),
        g_ple=row(w['g_ple'][l]), w_pe=w['w_pe'][l].astype(BF16), w_pg=w['w_pg'][l].astype(BF16),
    )


def _rope_tables(pos):
    def tab(rot, period):
        inv = ROPE_THETA ** (-jnp.arange(0, rot, 2, dtype=F32) / rot)
        ang = pos.astype(F32)[:, None] * inv[None, :]
        c, s = jnp.cos(ang), jnp.sin(ang)
        npass = period - rot
        cos = jnp.concatenate([c, c, jnp.ones((pos.shape[0], npass), F32)], axis=1)
        sin = jnp.concatenate([-s, s, jnp.zeros((pos.shape[0], npass), F32)], axis=1)
        return jnp.tile(cos, (1, LANE // period)), jnp.tile(sin, (1, LANE // period))
    ca, sa = tab(DR_A, LANE)
    cb, sb = tab(ROT_B, HD_B)
    return ca, sa, cb, sb


def _head_major(a, b, t, nh, d):
    return a.reshape(b, t, nh, d).transpose(0, 2, 1, 3).reshape(b, nh * t, d)


def _gla_state_out(st, b):
    return st.reshape(b, DV_C, H_C, DK_C).transpose(0, 2, 3, 1)


def _layer(xp, xs, pp, ps, lw, l, caches, s_prev, pt_flat, tabs_p, tabs_s, dims):
    bp, tp, bs, ts, n_pages = dims
    cache_ckv, krope_t, k_t, v_t, kidx_t = caches
    np_, ns = bp * tp, bs * ts
    tm_p, tm_s = min(256, np_), min(256, ns)

    (q, kcat, c, kr, kinv, ct, bq, bk, bkb, bv, bvt, iq, ik, ikb, iwt) = _front_ab(xp, tabs_p, lw, tm_p)
    cq, ck, cv, la, cr, gates = _front_cg(xp, lw, tm_p)
    oa = _mla_prompt(q, kcat, ct, kinv, lw['wuv_bd'], bp, tp)
    ob = _dsa_prompt(bq, iq, iwt, ikb, bkb, bvt, bp, tp)
    oc, st = _gla(cq, ck, cv, la, jnp.zeros((bp, DV_C, H_C * DK_C), F32), bp, tp)
    x1 = _back_merge(xp, oa, ob, oc, cr, gates, lw, tm_p)
    yp = _back_ffn(x1, pp, lw, min(512, np_))
    rows_p = (c.reshape(bp, tp, R_KV), kr.reshape(bp, tp, DR_A), bk.reshape(bp, tp, KVH_B, HD_B),
              bv.reshape(bp, tp, KVH_B, HD_B), ik.reshape(bp, tp, DI_B), _gla_state_out(st, bp))

    (q, kcat, c, kr, kinv, ct, bq, bk, bkb, bv, bvt, iq, ik, ikb, iwt) = _front_ab(xs, tabs_s, lw, tm_s)
    cq, ck, cv, la, cr, gates = _front_cg(xs, lw, tm_s)
    qs = _head_major(q, bs, ts, H_A, 256)
    iqs = _head_major(iq, bs, ts, HI_B, DI_B)
    iws = jnp.broadcast_to(iwt[:HI_B].reshape(HI_B, bs, ts).transpose(1, 0, 2).reshape(bs, HI_B * ts, 1),
                           (bs, HI_B * ts, LANE))
    kinvn = jnp.pad(kinv[:, ::LANE].reshape(bs, ts, H_A).transpose(0, 2, 1), ((0, 0), (0, 0), (0, PAGE - ts)))
    oa, keys, keysn = _mla_decode(pt_flat, qs, iqs, iws, c.reshape(bs, ts, R_KV), kr.reshape(bs, ts, DR_A),
                                  kinvn, ik.reshape(bs, ts, DI_B), lw, cache_ckv, krope_t, kidx_t,
                                  bs, ts, n_pages)
    std_order = [QHEAD_ORDER.index(h) for h in range(H_B)]
    bqs = bq.reshape(bs, ts, H_B, HD_B)[:, :, std_order, :].transpose(0, 2, 1, 3).reshape(bs, H_B * ts, HD_B)
    ob = _dsa_decode(pt_flat, keys, keysn, bqs, bk.reshape(bs, ts, 128), bv.reshape(bs, ts, 128),
                     k_t, v_t, bs, ts, n_pages)
    s0t = s_prev.transpose(0, 3, 1, 2).reshape(bs, DV_C, H_C * DK_C)
    oc, st = _gla(cq, ck, cv, la, s0t, bs, ts)
    x1 = _back_merge(xs, oa.reshape(ns, 512), ob.reshape(ns, 512), oc, cr, gates, lw, tm_s)
    ys = _back_ffn(x1, ps, lw, min(512, ns))
    rows_s = (c.reshape(bs, ts, R_KV), kr.reshape(bs, ts, DR_A), bk.reshape(bs, ts, KVH_B, HD_B),
              bv.reshape(bs, ts, KVH_B, HD_B), ik.reshape(bs, ts, DI_B), _gla_state_out(st, bs))
    return yp, ys, rows_p, rows_s


def kernel(x_prompt, x_sample, cache_ckv, cache_krope, cache_k, cache_v, cache_kidx, state_gla, page_table,
           p_prompt, p_sample, g_attn, w_in, g_qa, w_uq, g_q_a, g_kva, w_uk, w_uv, g_q_b, g_k_b, w_a2, b_a,
           g_o_c, w_pa, w_pb, w_pc, w_o, g_ffn, w_up, w_down, g_ple, w_pe, w_pg):
    w = dict(g_attn=g_attn, w_in=w_in, g_qa=g_qa, w_uq=w_uq, g_q_a=g_q_a, g_kva=g_kva, w_uk=w_uk, w_uv=w_uv,
             g_q_b=g_q_b, g_k_b=g_k_b, w_a2=w_a2, b_a=b_a, g_o_c=g_o_c, w_pa=w_pa, w_pb=w_pb, w_pc=w_pc,
             w_o=w_o, g_ffn=g_ffn, w_up=w_up, w_down=w_down, g_ple=g_ple, w_pe=w_pe, w_pg=w_pg)
    depth = w_in.shape[0]
    bp, tp, _ = x_prompt.shape
    bs, ts, _ = x_sample.shape
    n_pages = page_table.shape[1]
    past = n_pages * PAGE
    np_, ns = bp * tp, bs * ts
    tabs_p = _rope_tables(jnp.arange(tp))
    tm_s = min(256, ns)
    tabs_s = _rope_tables(past + (jnp.arange(tm_s) % ts))
    pt_flat = page_table.reshape(-1)
    xp, xs = x_prompt.reshape(np_, D_MODEL), x_sample.reshape(ns, D_MODEL)
    pages = lambda a: a.reshape((-1,) + a.shape[2:])
    caches = (pages(cache_ckv), pages(jnp.swapaxes(cache_krope, 2, 3)),
              pages(jnp.transpose(cache_k, (0, 1, 3, 4, 2))), pages(jnp.transpose(cache_v, (0, 1, 3, 4, 2))),
              pages(jnp.swapaxes(cache_kidx, 2, 3)))
    n_phys = cache_ckv.shape[1]
    new_p, new_s = [], []
    for l in range(depth):
        lw = _prep_layer(w, l)
        xp, xs, rows_p, rows_s = _layer(
            xp, xs, p_prompt[l].reshape(np_, D_PLE), p_sample[l].reshape(ns, D_PLE), lw, l,
            caches, state_gla[l], pt_flat + l * n_phys, tabs_p, tabs_s,
            (bp, tp, bs, ts, n_pages))
        new_p.append(rows_p)
        new_s.append(rows_s)
    stack = lambda rows, i: jnp.stack([r[i] for r in rows], axis=0)
    return ((xp.reshape(bp, tp, D_MODEL), xs.reshape(bs, ts, D_MODEL))
            + tuple(stack(new_p, i) for i in range(6)) + tuple(stack(new_s, i) for i in range(6)))
```

```python
import functools

import numpy as np
import jax
import jax.numpy as jnp
from jax import lax
from jax.experimental import pallas as pl
from jax.experimental.pallas import tpu as pltpu

D_MODEL = 1024
PAGE = 128
H_A, DN_A, DR_A, DV_A, R_Q, R_KV = 8, 64, 32, 64, 256, 128
SCALE_A = (DN_A + DR_A) ** -0.5
H_B, KVH_B, HD_B = 8, 2, 64
ROT_B = HD_B // 4
HI_B, DI_B = 4, 64
TOPK_MAX = 256
H_C, DK_C, DV_C, R_GATE_C = 4, 64, 128, 16
GLA_TAU = 16.0
GLA_BLOCK = 64
DECODE_PAGES = 32
ONES_ROWS = 16
V_ROWS = HD_B + ONES_ROWS
C_ROWS = R_KV + ONES_ROWS
QHEAD_ORDER = (0, 4, 1, 5, 2, 6, 3, 7)
D_FF = 4 * D_MODEL
D_PLE = 256
ROPE_THETA = 500000.0
EPS = 1e-6

N_A = R_Q + R_KV + DR_A
N_B = H_B * HD_B + 2 * KVH_B * HD_B + HI_B * DI_B + DI_B + HI_B
N_C = 2 * H_C * DK_C + 2 * H_C * DV_C + R_GATE_C
W_A, W_B, W_C = 512, 1152, 1664
LANE = 128
NEG = -1e30
INT_MIN = -(2 ** 31)

F32 = jnp.float32
BF16 = jnp.bfloat16
VMEM_LIMIT = 48 * 1024 * 1024


def _dot(a, b):
    return jnp.dot(a, b, preferred_element_type=F32)


def _dot_nt(a, b):
    return lax.dot_general(a, b, (((1,), (1,)), ((), ())), preferred_element_type=F32)


def _dot_tn(a, b):
    return lax.dot_general(a, b, (((0,), (0,)), ((), ())), preferred_element_type=F32)


def _split3(a):
    a1 = a.astype(BF16)
    r1 = a - a1.astype(F32)
    a2 = r1.astype(BF16)
    a3 = (r1 - a2.astype(F32)).astype(BF16)
    return a1, a2, a3


def _dot_f32(a, b_bf):
    a1, a2, a3 = _split3(a)
    return _dot(a1, b_bf) + _dot(a2, b_bf) + _dot(a3, b_bf)


def _rms(x, g):
    return x * lax.rsqrt(jnp.mean(x * x, axis=-1, keepdims=True) + EPS) * g


def _seg_inv(x, e, et, n):
    inv = lax.rsqrt(_dot_f32(x * x, e) * (1.0 / n) + EPS)
    return _dot_f32(inv, et)


def _rope(x, cos, sin, period, half):
    w = x.shape[-1]
    reps = w // LANE
    if reps > 1:
        cos = jnp.concatenate([cos] * reps, axis=1)
        sin = jnp.concatenate([sin] * reps, axis=1)
    lane = lax.broadcasted_iota(jnp.int32, x.shape, 1)
    first = (lane % period) < half
    partner = jnp.where(first, pltpu.roll(x, w - half, 1), pltpu.roll(x, half, 1))
    return x * cos + partner * sin


def _sigmoid(x):
    return 1.0 / (1.0 + jnp.exp(-x))


def _front_ab_kernel(x_ref, g_ref, wa_ref, wb_ref, gqa_ref, wq_ref, gn_ref, gr_ref, gkv_ref, wuk_ref,
                     wabs_ref, gqb_ref, gkb_ref, ca_ref, sa_ref, cb_ref, sb_ref,
                     e64_ref, e64t_ref, e128_ref, e128t_ref, e64x_ref,
                     q_ref, kcat_ref, c_ref, kr_ref, kinv_ref, ct_ref,
                     bq_ref, bk_ref, bkb_ref, bv_ref, bvt_ref, iq_ref, ik_ref, ikb_ref, iwt_ref):
    xn = _rms(x_ref[...], g_ref[...]).astype(BF16)
    za = _dot_nt(xn, wa_ref[...])
    zb = _dot_nt(xn, wb_ref[...])
    ca, sa, cb, sb = ca_ref[...], sa_ref[...], cb_ref[...], sb_ref[...]
    e64, e64t = e64_ref[...], e64t_ref[...]

    aq = _rms(za[:, :R_Q], gqa_ref[...]).astype(BF16)
    q2 = _dot(aq, wq_ref[...])
    qn, qr = q2[:, :512], q2[:, 512:]
    ss = _dot_f32(qn * qn, e64) + _dot_f32(qr * qr, e128_ref[...])
    inv = lax.rsqrt(ss * (1.0 / (DN_A + DR_A)) + EPS)
    qn = qn * _dot_f32(inv, e64t) * gn_ref[...]
    qr = qr * _dot_f32(inv, e128t_ref[...]) * gr_ref[...]
    qr = _rope(qr, ca, sa, LANE, DR_A // 2) * SCALE_A
    qlat = _dot(qn.astype(BF16), wabs_ref[...])
    for h in range(H_A):
        q_ref[:, h * 256:h * 256 + 128] = qlat[:, h * 128:(h + 1) * 128].astype(BF16)
        q_ref[:, h * 256 + 128:(h + 1) * 256] = qr[:, h * 128:(h + 1) * 128].astype(BF16)

    c = _rms(za[:, R_Q:R_Q + R_KV], gkv_ref[...])
    kr = _rope(za[:, R_Q + R_KV:], ca, sa, LANE, DR_A // 2)
    c_ref[...] = c
    kr_ref[...] = kr[:, :DR_A]
    cb16 = c.astype(BF16)
    kcat_ref[:, :R_KV] = cb16
    kcat_ref[:, R_KV:] = kr.astype(BF16)
    ones = jnp.ones((ONES_ROWS, c.shape[0]), F32)
    ct_ref[...] = jnp.concatenate([c.T, ones], axis=0).astype(BF16)
    kn = _dot(cb16, wuk_ref[...])
    ms = _dot_f32(kn * kn, e64x_ref[...]) + jnp.sum(kr * kr, axis=-1, keepdims=True)
    kinv_ref[...] = lax.rsqrt(ms * (1.0 / (DN_A + DR_A)) + EPS)

    bq = zb[:, :512]
    bq = bq * _seg_inv(bq, e64, e64t, HD_B) * gqb_ref[...]
    bq_ref[...] = (_rope(bq, cb, sb, HD_B, ROT_B // 2) * (HD_B ** -0.5)).astype(BF16)
    bk = zb[:, 512:640]
    bk = bk * _seg_inv(bk, e64[:128], e64t[:, :128], HD_B) * gkb_ref[...]
    bk = _rope(bk, cb, sb, HD_B, ROT_B // 2)
    bk_ref[...] = bk
    bkb_ref[...] = bk.astype(BF16)
    bv = zb[:, 640:768]
    bv_ref[...] = bv
    bvt = bv.T
    bvt_ref[...] = jnp.concatenate([bvt[:HD_B], ones, bvt[HD_B:], ones], axis=0).astype(BF16)
    iq_ref[...] = (_rope(zb[:, 768:1024], cb, sb, DI_B, ROT_B // 2) * (DI_B ** -0.5)).astype(BF16)
    last = zb[:, 1024:1152]
    ik = _rope(last, cb, sb, DI_B, ROT_B // 2)[:, :DI_B]
    ik_ref[...] = ik
    ikb_ref[...] = ik.astype(BF16)
    iwt_ref[...] = (last * (HI_B ** -0.5)).T[DI_B:DI_B + 8, :]


def _front_ab(x2d, tabs, lw, tm):
    n = x2d.shape[0]
    nblk = tabs[0].shape[0] // tm
    row = lambda w: pl.BlockSpec((tm, w), lambda i: (i, 0))
    full = lambda a: pl.BlockSpec(a.shape, lambda i: (0,) * a.ndim)
    tab = pl.BlockSpec((tm, LANE), lambda i: (i % nblk, 0))
    consts = [lw['g_attn'], lw['wa'], lw['wb'], lw['g_qa'], lw['wq2'], lw['gn'], lw['gr'], lw['g_kva'],
              lw['wuk'], lw['wabs'], lw['gqb'], lw['gkb']]
    inds = [lw['e64'], lw['e64t'], lw['e128'], lw['e128t'], lw['e64x']]
    outs = [(2048, BF16, None), (256, BF16, None), (R_KV, F32, None), (DR_A, F32, None),
            (H_A * LANE, F32, None), (C_ROWS, BF16),
            (512, BF16, None), (128, F32, None), (128, BF16, None), (128, F32, None), (KVH_B * V_ROWS, BF16),
            (256, BF16, None), (DI_B, F32, None), (DI_B, BF16, None), (8, F32)]
    out_shape, out_specs = [], []
    for o in outs:
        if len(o) == 2:
            out_shape.append(jax.ShapeDtypeStruct((o[0], n), o[1]))
            out_specs.append(pl.BlockSpec((o[0], tm), lambda i: (0, i)))
        else:
            out_shape.append(jax.ShapeDtypeStruct((n, o[0]), o[1]))
            out_specs.append(row(o[0]))
    return pl.pallas_call(
        _front_ab_kernel, grid=(n // tm,),
        in_specs=[row(D_MODEL)] + [full(a) for a in consts] + [tab] * 4 + [full(a) for a in inds],
        out_specs=out_specs, out_shape=out_shape, name='front_ab',
        compiler_params=pltpu.CompilerParams(dimension_semantics=('arbitrary',), vmem_limit_bytes=VMEM_LIMIT),
    )(x2d, *consts, *tabs, *inds)


def _front_cg_kernel(x_ref, g_ref, wc_ref, wg_ref, wa2_ref, ba_ref,
                     cq_ref, ck_ref, cv_ref, la_ref, cr_ref, gate_ref):
    xn = _rms(x_ref[...], g_ref[...]).astype(BF16)
    zc = _dot_nt(xn, wc_ref[...])
    cq_ref[...] = zc[:, :256]
    ck_ref[...] = zc[:, 256:512] * (DK_C ** -0.5)
    cv_ref[...] = zc[:, 512:1024]
    cr_ref[...] = zc[:, 1024:1536]
    u = _dot(zc[:, 1536:1664].astype(BF16), wa2_ref[...]) + ba_ref[...]
    la_ref[...] = (jnp.minimum(u, 0.0) - jnp.log1p(jnp.exp(-jnp.abs(u)))) * (1.0 / GLA_TAU)
    gate_ref[...] = _dot_nt(xn, wg_ref[...])


def _front_cg(x2d, lw, tm):
    n = x2d.shape[0]
    row = lambda w: pl.BlockSpec((tm, w), lambda i: (i, 0))
    full = lambda a: pl.BlockSpec(a.shape, lambda i: (0,) * a.ndim)
    consts = [lw['g_attn'], lw['wc'], lw['wg'], lw['wa2'], lw['b_a']]
    widths = [256, 256, 512, 256, 512, 3 * D_MODEL]
    return pl.pallas_call(
        _front_cg_kernel, grid=(n // tm,),
        in_specs=[row(D_MODEL)] + [full(a) for a in consts],
        out_specs=[row(w) for w in widths],
        out_shape=[jax.ShapeDtypeStruct((n, w), F32) for w in widths], name='front_cg',
        compiler_params=pltpu.CompilerParams(dimension_semantics=('arbitrary',), vmem_limit_bytes=VMEM_LIMIT),
    )(x2d, *consts)


def _mla_prompt_kernel(qi_ref, kj_ref, q_ref, k_ref, ct_ref, kinv_ref, wuv_ref, o_ref, m_ref, acc_ref, *, tq):
    i, j = qi_ref[pl.program_id(1)], kj_ref[pl.program_id(1)]

    @pl.when(j == 0)
    def _():
        m_ref[...] = jnp.full(m_ref.shape, NEG, F32)
        acc_ref[...] = jnp.zeros(acc_ref.shape, F32)

    def step(diagonal):
        k = k_ref[...]
        ct = ct_ref[...]
        if diagonal:
            visible = (lax.broadcasted_iota(jnp.int32, (tq, tq), 0) <= lax.broadcasted_iota(jnp.int32, (tq, tq), 1))
        def logits(h):
            kinv = kinv_ref[:, h * LANE:(h + 1) * LANE]
            s = _dot_nt(k, q_ref[:, h * 256:(h + 1) * 256]) * jnp.concatenate([kinv] * (tq // LANE), axis=1)
            return jnp.where(visible, s, NEG) if diagonal else s

        s_next = logits(0)
        for h in range(H_A):
            s = s_next
            if h + 1 < H_A:
                s_next = logits(h + 1)
            m_prev = m_ref[h:h + 1, :]
            m_new = jnp.maximum(m_prev, jnp.max(s, axis=0, keepdims=True))
            p = jnp.exp(s - m_new).astype(BF16)
            acc_ref[h] = jnp.exp(m_prev - m_new) * acc_ref[h] + _dot(ct, p)
            m_ref[h:h + 1, :] = m_new

    @pl.when(j < i)
    def _():
        step(False)

    @pl.when(j == i)
    def _():
        step(True)
        olat_t = jnp.concatenate([acc_ref[h, :R_KV, :] / acc_ref[h, R_KV:R_KV + 1, :] for h in range(H_A)],
                                 axis=0)
        o_ref[...] = _dot(olat_t.T.astype(BF16), wuv_ref[...]).astype(BF16)


def _mla_prompt(q, kcat, ct, kinv, wuv_bd, b, t):
    tq = min(256, t)
    nq = t // tq
    pairs = [(i, j) for i in range(nq) for j in range(i + 1)]
    qi = jnp.asarray([p[0] for p in pairs], jnp.int32)
    kj = jnp.asarray([p[1] for p in pairs], jnp.int32)
    qrow = lambda bb, p, qi, kj: (bb * nq + qi[p], 0)
    krow = lambda bb, p, qi, kj: (bb * nq + kj[p], 0)
    grid_spec = pltpu.PrefetchScalarGridSpec(
        num_scalar_prefetch=2, grid=(b, len(pairs)),
        in_specs=[pl.BlockSpec((tq, 2048), qrow), pl.BlockSpec((tq, 256), krow),
                  pl.BlockSpec((C_ROWS, tq), lambda bb, p, qi, kj: (0, bb * nq + kj[p])),
                  pl.BlockSpec((tq, H_A * LANE), krow),
                  pl.BlockSpec(wuv_bd.shape, lambda bb, p, qi, kj: (0, 0))],
        out_specs=pl.BlockSpec((tq, 512), qrow),
        scratch_shapes=[pltpu.VMEM((H_A, tq), F32), pltpu.VMEM((H_A, C_ROWS, tq), F32)])
    return pl.pallas_call(
        functools.partial(_mla_prompt_kernel, tq=tq), grid_spec=grid_spec,
        out_shape=jax.ShapeDtypeStruct((b * t, 512), BF16), name='mla_prompt',
        compiler_params=pltpu.CompilerParams(dimension_semantics=('arbitrary', 'arbitrary'),
                                             vmem_limit_bytes=VMEM_LIMIT),
    )(qi, kj, q, kcat, ct, kinv, wuv_bd)


def _score_key(sc):
    sc = jnp.where(sc == 0.0, 0.0, sc)
    bits = lax.bitcast_convert_type(sc, jnp.int32)
    return jnp.where(bits < 0, bits ^ 0x7FFFFFFF, bits)


def _kth_largest(count_ge, k, shape):
    base = jnp.where(count_ge(jnp.zeros(shape, jnp.int32)) >= k, 0, INT_MIN).astype(jnp.int32)

    def body(it, base):
        cand = base + jnp.left_shift(jnp.int32(1), 30 - it)
        return jnp.where(count_ge(cand) >= k, cand, base)
    return lax.fori_loop(0, 31, body, base)


def _cumsum_lanes(x):
    r, w = x.shape
    nb = w // LANE
    xs = jnp.concatenate([x[:, j * LANE:(j + 1) * LANE] for j in range(nb)], axis=0) if nb > 1 else x
    a = lax.broadcasted_iota(jnp.int32, (LANE, LANE), 0)
    b = lax.broadcasted_iota(jnp.int32, (LANE, LANE), 1)
    p = _dot(xs.astype(BF16), jnp.where(a <= b, 1.0, 0.0).astype(BF16))
    if nb == 1:
        return p
    ra = lax.broadcasted_iota(jnp.int32, (nb * r, nb * r), 0)
    rb = lax.broadcasted_iota(jnp.int32, (nb * r, nb * r), 1)
    earlier = jnp.where((rb // r < ra // r) & (rb % r == ra % r), 1.0, 0.0).astype(BF16)
    tot = jnp.broadcast_to(p[:, LANE - 1:LANE], (nb * r, LANE)).astype(BF16)
    p = p + _dot(earlier, tot)
    return jnp.concatenate([p[j * r:(j + 1) * r, :] for j in range(nb)], axis=1)


def _select(keys, kth, need, carry):
    eq = (keys == kth).astype(F32)
    pref = _cumsum_lanes(eq) - eq + carry
    sel = (keys > kth) | ((eq > 0.0) & (pref < need))
    return sel, carry + jnp.sum(eq, axis=-1, keepdims=True)


def _dsa_prompt_kernel(bq_ref, iq_ref, iwt_ref, ik_ref, k_ref, vt_ref, o_ref, keys_ref, bias_ref,
                       *, tq, tkb, nkb, tile0, n_sel):
    i = pl.program_id(1)
    blocks = [slice(kb * tkb, (kb + 1) * tkb) for kb in range(nkb)]
    keyp = (nkb - 1) * tkb + lax.broadcasted_iota(jnp.int32, (tkb, tq), 0)
    visible = keyp <= (tile0 + i) * tq + lax.broadcasted_iota(jnp.int32, (tkb, tq), 1)
    iwt = iwt_ref[...]
    iq = [iq_ref[:, h * DI_B:(h + 1) * DI_B] for h in range(HI_B)]

    for kb, blk in enumerate(blocks):
        ik = ik_ref[blk, :]
        sc = jnp.zeros((tkb, tq), F32)
        for h in range(HI_B):
            sc = sc + jnp.maximum(_dot_nt(ik, iq[h]), 0.0) * iwt[h:h + 1, :]
        if kb == nkb - 1:
            sc = jnp.where(visible, sc, -jnp.inf)
        keys_ref[blk, :] = _score_key(sc)

    def count(pred):
        acc = jnp.zeros((1, tq), F32)
        for blk in blocks:
            acc = acc + jnp.sum(jnp.where(pred(keys_ref[blk, :]), 1.0, 0.0), axis=0, keepdims=True)
        return acc

    kth = _kth_largest(lambda cand: count(lambda x: x >= cand), float(n_sel), (1, tq))
    need = float(n_sel) - count(lambda x: x > kth)

    lower = jnp.where(lax.broadcasted_iota(jnp.int32, (tkb, tkb), 1) < lax.broadcasted_iota(jnp.int32, (tkb, tkb), 0),
                      1.0, 0.0).astype(BF16)
    ties = jnp.zeros((1, tq), F32)
    for kb, blk in enumerate(blocks):
        keys = keys_ref[blk, :]
        eq = keys == kth
        eqf = jnp.where(eq, 1.0, 0.0)
        before = _dot(lower, eqf.astype(BF16)) + ties
        sel = (keys > kth) | (eq & (before < need))
        if kb == nkb - 1:
            sel = sel & visible
        bias_ref[blk, :] = jnp.where(sel, 0.0, NEG)
        ties = ties + jnp.sum(eqf, axis=0, keepdims=True)

    upper_half = lax.broadcasted_iota(jnp.int32, (tq, LANE), 1) >= HD_B

    def logits(pos):
        qb = bq_ref[:, (pos // 2) * LANE:(pos // 2 + 1) * LANE]
        qpad = jnp.where(upper_half == (pos % 2 == 1), qb, jnp.zeros_like(qb))
        return [_dot_nt(k_ref[blk, :], qpad) + bias_ref[blk, :] for blk in blocks]

    outs = [None] * H_B
    s_next = logits(0)
    for pos in range(H_B):
        s = s_next
        if pos + 1 < H_B:
            s_next = logits(pos + 1)
        n = pos % 2
        m = functools.reduce(jnp.maximum, [jnp.max(sb, axis=0, keepdims=True) for sb in s])
        acc = jnp.zeros((V_ROWS, tq), F32)
        for sb, blk in zip(s, blocks):
            acc = acc + _dot(vt_ref[n * V_ROWS:(n + 1) * V_ROWS, blk], jnp.exp(sb - m).astype(BF16))
        outs[QHEAD_ORDER[pos]] = acc[:HD_B, :] / acc[HD_B:HD_B + 1, :]
    o_ref[0] = jnp.concatenate(outs, axis=0).T.astype(BF16)


def _dsa_prompt(bq, iq, iwt, ikb, bkb, bvt, b, t):
    tq = min(512, t)
    tkb = min(512, t)
    nq = t // tq
    tpc = tkb // tq
    n_sel = min(TOPK_MAX, t // 4)
    seq = lambda w: pl.BlockSpec((t, w), lambda bb, i: (bb, 0))
    outs = []
    for cls in range(t // tkb):
        nkb = cls + 1
        qrow = lambda w, cls=cls: pl.BlockSpec((tq, w), lambda bb, i: (bb * nq + cls * tpc + i, 0))
        outs.append(pl.pallas_call(
            functools.partial(_dsa_prompt_kernel, tq=tq, tkb=tkb, nkb=nkb, tile0=cls * tpc, n_sel=n_sel),
            grid=(b, tpc),
            in_specs=[qrow(512), qrow(256),
                      pl.BlockSpec((8, tq), lambda bb, i, cls=cls: (0, bb * nq + cls * tpc + i)),
                      seq(DI_B), seq(128), pl.BlockSpec((KVH_B * V_ROWS, t), lambda bb, i: (0, bb))],
            out_specs=pl.BlockSpec((1, tq, 512), lambda bb, i: (bb, i, 0)),
            out_shape=jax.ShapeDtypeStruct((b, tkb, 512), BF16),
            scratch_shapes=[pltpu.VMEM((nkb * tkb, tq), jnp.int32), pltpu.VMEM((nkb * tkb, tq), F32)],
            name='dsa_prompt',
            compiler_params=pltpu.CompilerParams(dimension_semantics=('arbitrary', 'arbitrary'),
                                                 vmem_limit_bytes=VMEM_LIMIT),
        )(bq, iq, iwt, ikb, bkb, bvt))
    return jnp.concatenate(outs, axis=1).reshape(b * t, 512)


def _gla_kernel(q_ref, k_ref, v_ref, la_ref, s0_ref, o_ref, sout_ref, st_ref, *, c, nsub):
    j = pl.program_id(1)

    @pl.when(j == 0)
    def _():
        st_ref[...] = s0_ref[0]

    kw = H_C * DK_C
    r = lax.broadcasted_iota(jnp.int32, (c, c), 0)
    cc = lax.broadcasted_iota(jnp.int32, (c, c), 1)
    tril = jnp.where(r >= cc, 1.0, 0.0).astype(BF16)
    row = lax.broadcasted_iota(jnp.int32, (c, kw), 0)
    head = lax.broadcasted_iota(jnp.int32, (c, kw), 1) // DK_C

    def heads(a):
        return [jnp.where(head == h, a, 0.0).astype(BF16) for h in range(H_C)]

    def body(i, carry):
        off = pl.multiple_of(i * c, c)
        la = la_ref[0, pl.ds(off, c), :]
        l1, l2, l3 = _split3(la)
        lb = _dot(tril, l1) + _dot(tril, l2) + _dot(tril, l3)
        q = q_ref[0, pl.ds(off, c), :]
        k = k_ref[0, pl.ds(off, c), :]
        v = v_ref[0, pl.ds(off, c), :]

        k16 = k.astype(BF16)
        att = [jnp.where(r == cc, _dot_nt(qh, k16), 0.0) for qh in heads(q)]
        fl = lb
        b = 1
        while b < c:
            up = (row % (2 * b)) >= b
            if b > 1:
                fl = jnp.where((row % b) >= b // 2, pltpu.roll(fl, b // 2, 0), fl)
            qf = jnp.where(up, q * jnp.exp(jnp.where(up, lb - fl, 0.0)), 0.0)
            nxt = pltpu.roll(fl, c - b, 0)
            kf = jnp.where(up, 0.0, k * jnp.exp(jnp.where(up, 0.0, nxt - lb))).astype(BF16)
            same = (r // (2 * b)) == (cc // (2 * b))
            att = [a + jnp.where(same, _dot_nt(qh, kf), 0.0) for a, qh in zip(att, heads(qf))]
            b *= 2

        st = st_ref[...]
        st16 = st.astype(BF16)
        lb_end = lb[c - 1:c, :]
        upd = jnp.zeros(st.shape, F32)
        for h, (qe, kt) in enumerate(zip(heads(q * jnp.exp(lb)), heads(k * jnp.exp(lb_end - lb)))):
            vh = v[:, h * DV_C:(h + 1) * DV_C].astype(BF16)
            o_ref[0, pl.ds(off, c), h * DV_C:(h + 1) * DV_C] = _dot(att[h].astype(BF16), vh) + _dot_nt(qe, st16)
            upd = upd + _dot_tn(vh, kt)
        st_ref[...] = st * jnp.exp(lb_end) + upd
        return carry
    lax.fori_loop(0, nsub, body, 0)

    @pl.when(j == pl.num_programs(1) - 1)
    def _():
        sout_ref[0] = st_ref[...]


def _gla(cq, ck, cv, la, s0t, b, t):
    c = min(GLA_BLOCK, t)
    tt = min(512, t)
    r3 = lambda a: a.reshape(b, t, a.shape[-1])
    tok = lambda w: pl.BlockSpec((1, tt, w), lambda bb, j: (bb, j, 0))
    st = pl.BlockSpec((1, DV_C, H_C * DK_C), lambda bb, j: (bb, 0, 0))
    oc, s_out = pl.pallas_call(
        functools.partial(_gla_kernel, c=c, nsub=tt // c), grid=(b, t // tt),
        in_specs=[tok(256), tok(256), tok(512), tok(256), st],
        out_specs=[tok(512), st],
        out_shape=[jax.ShapeDtypeStruct((b, t, 512), F32), jax.ShapeDtypeStruct((b, DV_C, H_C * DK_C), F32)],
        scratch_shapes=[pltpu.VMEM((DV_C, H_C * DK_C), F32)], name='gla',
        compiler_params=pltpu.CompilerParams(dimension_semantics=('arbitrary', 'arbitrary'),
                                             vmem_limit_bytes=VMEM_LIMIT),
    )(r3(cq), r3(ck), r3(cv), r3(la), s0t)
    return oc.reshape(b * t, 512), s_out


def _back_merge_kernel(x_ref, oa_ref, ob_ref, oc_ref, cr_ref, gate_ref, goc_ref,
                       wpa_ref, wpb_ref, wpc_ref, wo_ref, y_ref):
    oc, cr = oc_ref[...], cr_ref[...]
    parts = []
    for h in range(H_C):
        och = oc[:, h * DV_C:(h + 1) * DV_C]
        parts.append(och * lax.rsqrt(jnp.mean(och * och, axis=-1, keepdims=True) + EPS))
    ocn = jnp.concatenate(parts, axis=1) * goc_ref[...] * (cr * _sigmoid(cr))
    gate = gate_ref[...]
    merged = (_sigmoid(gate[:, :D_MODEL]) * _dot(oa_ref[...], wpa_ref[...])
              + _sigmoid(gate[:, D_MODEL:2 * D_MODEL]) * _dot(ob_ref[...], wpb_ref[...])
              + _sigmoid(gate[:, 2 * D_MODEL:]) * _dot(ocn.astype(BF16), wpc_ref[...]))
    y_ref[...] = x_ref[...] + _dot(merged.astype(BF16), wo_ref[...])


def _back_merge(x2d, oa, ob, oc, cr, gates, lw, tm):
    n = x2d.shape[0]
    row = lambda w: pl.BlockSpec((tm, w), lambda i: (i, 0))
    full = lambda a: pl.BlockSpec(a.shape, lambda i: (0,) * a.ndim)
    consts = [lw['goc'], lw['w_pa'], lw['w_pb'], lw['w_pc'], lw['w_o']]
    return pl.pallas_call(
        _back_merge_kernel, grid=(n // tm,),
        in_specs=[row(D_MODEL), row(512), row(512), row(512), row(512), row(3 * D_MODEL)]
        + [full(a) for a in consts],
        out_specs=row(D_MODEL), out_shape=jax.ShapeDtypeStruct((n, D_MODEL), F32), name='back_merge',
        compiler_params=pltpu.CompilerParams(dimension_semantics=('arbitrary',), vmem_limit_bytes=VMEM_LIMIT),
    )(x2d, oa, ob, oc, cr, gates, *consts)


def _back_ffn_kernel(x_ref, p_ref, gf_ref, wup_ref, wdn_ref, gp_ref, wpg_ref, wpe_ref, y_ref, xn_ref, acc_ref):
    f = pl.program_id(1)

    @pl.when(f == 0)
    def _():
        xn_ref[...] = _rms(x_ref[...], gf_ref[...]).astype(BF16)
        acc_ref[...] = x_ref[...]

    hid = jnp.maximum(_dot(xn_ref[...], wup_ref[...]), 0.0)
    acc_ref[...] += _dot((hid * hid).astype(BF16), wdn_ref[...])

    @pl.when(f == pl.num_programs(1) - 1)
    def _():
        x2 = acc_ref[...]
        gate = _sigmoid(_dot(_rms(x2, gp_ref[...]).astype(BF16), wpg_ref[...]))
        y_ref[...] = x2 + gate * _dot(p_ref[...].astype(BF16), wpe_ref[...])


def _back_ffn(x2d, p2d, lw, tm):
    n = x2d.shape[0]
    tf = 1024
    row = lambda w: pl.BlockSpec((tm, w), lambda i, f: (i, 0))
    full = lambda a: pl.BlockSpec(a.shape, lambda i, f: (0,) * a.ndim)
    return pl.pallas_call(
        _back_ffn_kernel, grid=(n // tm, D_FF // tf),
        in_specs=[row(D_MODEL), row(D_PLE), full(lw['g_ffn']),
                  pl.BlockSpec((D_MODEL, tf), lambda i, f: (0, f)), pl.BlockSpec((tf, D_MODEL), lambda i, f: (f, 0)),
                  full(lw['g_ple']), full(lw['w_pg']), full(lw['w_pe'])],
        out_specs=row(D_MODEL), out_shape=jax.ShapeDtypeStruct((n, D_MODEL), F32),
        scratch_shapes=[pltpu.VMEM((tm, D_MODEL), BF16), pltpu.VMEM((tm, D_MODEL), F32)], name='back_ffn',
        compiler_params=pltpu.CompilerParams(dimension_semantics=('arbitrary', 'arbitrary'),
                                             vmem_limit_bytes=VMEM_LIMIT),
    )(x2d, p2d, lw['g_ffn'], lw['w_up'], lw['w_down'], lw['g_ple'], lw['w_pg'], lw['w_pe'])


def _page_copies(pt_ref, first, srcs, bufs, sems, slot, npg):
    copies = []
    for pg in range(npg):
        page = pt_ref[first + pg]
        for c, (src, buf) in enumerate(zip(srcs, bufs)):
            copies.append(pltpu.make_async_copy(src.at[page], buf.at[slot, pg], sems.at[slot, c]))
    return copies


def _fetch_pages(pt_ref, srcs, bufs, sems, npg):
    n = pl.program_id(0) * pl.num_programs(1) + pl.program_id(1)
    total = pl.num_programs(0) * pl.num_programs(1)
    slot = n % 2

    @pl.when(n == 0)
    def _():
        for cp in _page_copies(pt_ref, 0, srcs, bufs, sems, 0, npg):
            cp.start()

    for cp in _page_copies(pt_ref, n * npg, srcs, bufs, sems, slot, npg):
        cp.wait()

    @pl.when(n + 1 < total)
    def _():
        for cp in _page_copies(pt_ref, (n + 1) * npg, srcs, bufs, sems, 1 - slot, npg):
            cp.start()
    return slot


def _mla_decode_kernel(pt_ref, q_ref, iq_ref, iw_ref, cn_ref, krn_ref, kinvn_ref, ikn_ref,
                       wukt_ref, wuv_ref, ckv_hbm, krp_hbm, kix_hbm,
                       o_ref, keys_ref, keysn_ref, m_ref, l_ref, acc_ref, s_ref, c16_ref,
                       ckv_buf, krp_buf, kix_buf, sems, *, npg, cpg, t):
    slot = _fetch_pages(pt_ref, (ckv_hbm, krp_hbm, kix_hbm), (ckv_buf, krp_buf, kix_buf), sems, npg)
    ckv = [ckv_buf.at[slot, pg] for pg in range(npg)]
    krp = [krp_buf.at[slot, pg] for pg in range(npg)]
    kix = [kix_buf.at[slot, pg] for pg in range(npg)]
    g = pl.program_id(1)
    rows = H_A * t
    nup = H_A * DN_A

    @pl.when(g == 0)
    def _():
        m_ref[...] = jnp.full(m_ref.shape, NEG, F32)
        l_ref[...] = jnp.zeros(l_ref.shape, F32)
        acc_ref[...] = jnp.zeros(acc_ref.shape, F32)

    q = q_ref[0]
    qlat, qr = q[:, :R_KV], q[:, R_KV:R_KV + DR_A]
    lhs = jnp.concatenate([wukt_ref[...], qlat], axis=0)
    iq = iq_ref[0]
    iw = iw_ref[0]

    def softmax_step(s, v16):
        m_prev = m_ref[...]
        m_new = jnp.maximum(m_prev, jnp.max(s, axis=-1, keepdims=True))
        p = jnp.exp(s - m_new)
        alpha = jnp.exp(m_prev - m_new)
        l_ref[...] = alpha * l_ref[...] + jnp.sum(p, axis=-1, keepdims=True)
        acc_ref[...] = alpha * acc_ref[...] + _dot(p.astype(BF16), v16)
        m_ref[...] = m_new

    def per_head(s, kinv_rows):
        return jnp.concatenate([s[h * t:(h + 1) * t, :] * kinv_rows[h] for h in range(H_A)], axis=0)

    def combine_heads(d):
        sc = jnp.zeros((t, d.shape[1]), F32)
        for h in range(HI_B):
            sc = sc + d[h * t:(h + 1) * t, :] * iw[h * t:(h + 1) * t, :1]
        return sc

    nch = npg // cpg
    chunk_cols = [slice(ch * cpg * PAGE, (ch + 1) * cpg * PAGE) for ch in range(nch)]

    def products(ch):
        pages = range(ch * cpg, (ch + 1) * cpg)
        c16 = jnp.concatenate([ckv[pg][...].astype(BF16) for pg in pages], axis=0)
        c16_ref[chunk_cols[ch], :] = c16
        krt = jnp.concatenate([krp[pg][...] for pg in pages], axis=1)
        ikt = jnp.concatenate([kix[pg][...].astype(BF16) for pg in pages], axis=1)
        r = _dot_nt(lhs, c16)
        return r, _dot(qr, krt.astype(BF16)), jnp.sum(krt * krt, axis=0, keepdims=True), _dot(iq, ikt)

    nxt = products(0)
    for ch in range(nch):
        r, s_rope, kr2, d = nxt
        if ch + 1 < nch:
            nxt = products(ch + 1)
        kinv = []
        for h in range(H_A):
            knh = r[h * DN_A:(h + 1) * DN_A, :]
            ms = jnp.sum(knh * knh, axis=0, keepdims=True) + kr2
            kinv.append(lax.rsqrt(ms * (1.0 / (DN_A + DR_A)) + EPS))
        s_ref[:, chunk_cols[ch]] = per_head(r[nup:, :] + s_rope, kinv)
        keys_ref[0, :, chunk_cols[ch]] = _score_key(combine_heads(jnp.maximum(d, 0.0)))
    softmax_step(s_ref[...], c16_ref[...])

    @pl.when(g == pl.num_programs(1) - 1)
    def _():
        pad = lambda a: jnp.concatenate([a, jnp.zeros((PAGE - t, a.shape[1]), a.dtype)], axis=0)
        col = lax.broadcasted_iota(jnp.int32, (rows, PAGE), 1)
        qt = lax.broadcasted_iota(jnp.int32, (rows, PAGE), 0) % t
        cn16 = pad(cn_ref[0]).astype(BF16)
        kinvn = kinvn_ref[0]
        s = _dot_nt(qlat, cn16) + _dot_nt(qr, pad(krn_ref[0]).astype(BF16))
        s = per_head(s, [kinvn[h:h + 1, :] for h in range(H_A)])
        softmax_step(jnp.where(col <= qt, s, NEG), cn16)
        dn = jnp.maximum(_dot_nt(iq, pad(ikn_ref[0]).astype(BF16)), 0.0)
        scn = jnp.where((col <= qt)[:t], combine_heads(dn), -jnp.inf)
        keysn_ref[0] = _score_key(scn)
        o = acc_ref[...] / l_ref[...]
        olat = jnp.concatenate([o[h * t:(h + 1) * t, :] for h in range(H_A)], axis=1).astype(BF16)
        o_ref[0] = _dot(olat, wuv_ref[...]).astype(BF16)


def _mla_decode(pt_flat, qs, iqs, iws, cn, krn, kinvn, ikn, lw, cache_ckv, krope_t, kidx_t, b, t, n_pages):
    npg = min(DECODE_PAGES, n_pages)
    cpg = min(4, npg)
    ng = n_pages // npg
    past = n_pages * PAGE
    per_b = lambda a: pl.BlockSpec((1,) + a.shape[1:], lambda bb, g, pt: (bb,) + (0,) * (a.ndim - 1))
    full = lambda a: pl.BlockSpec(a.shape, lambda bb, g, pt: (0,) * a.ndim)

    small = [qs, iqs, iws, cn, krn, kinvn, ikn]
    consts = [lw['wukt'], lw['wuv_bd']]
    in_specs = ([per_b(a) for a in small] + [full(a) for a in consts]
                + [pl.BlockSpec(memory_space=pl.ANY)] * 3)
    grid_spec = pltpu.PrefetchScalarGridSpec(
        num_scalar_prefetch=1, grid=(b, ng), in_specs=in_specs,
        out_specs=[pl.BlockSpec((1, t, 512), lambda bb, g, pt: (bb, 0, 0)),
                   pl.BlockSpec((1, t, npg * PAGE), lambda bb, g, pt: (bb, 0, g)),
                   pl.BlockSpec((1, t, PAGE), lambda bb, g, pt: (bb, 0, 0))],
        scratch_shapes=[pltpu.VMEM((H_A * t, 1), F32), pltpu.VMEM((H_A * t, 1), F32),
                        pltpu.VMEM((H_A * t, R_KV), F32), pltpu.VMEM((H_A * t, npg * PAGE), F32),
                        pltpu.VMEM((npg * PAGE, R_KV), BF16),
                        pltpu.VMEM((2, npg, PAGE, R_KV), F32), pltpu.VMEM((2, npg, DR_A, PAGE), F32),
                        pltpu.VMEM((2, npg, DI_B, PAGE), F32), pltpu.SemaphoreType.DMA((2, 3))])
    return pl.pallas_call(
        functools.partial(_mla_decode_kernel, npg=npg, cpg=cpg, t=t), grid_spec=grid_spec,
        out_shape=[jax.ShapeDtypeStruct((b, t, 512), BF16), jax.ShapeDtypeStruct((b, t, past), jnp.int32),
                   jax.ShapeDtypeStruct((b, t, PAGE), jnp.int32)], name='mla_decode',
        compiler_params=pltpu.CompilerParams(dimension_semantics=('arbitrary', 'arbitrary'),
                                             vmem_limit_bytes=VMEM_LIMIT),
    )(pt_flat, *small, *consts, cache_ckv, krope_t, kidx_t)


def _threshold_kernel(keys_ref, keysn_ref, kth_ref, need_ref, tie_ref, *, n_sel):
    nb, t, past = keys_ref.shape
    keys = keys_ref[...].reshape(nb * t, past)
    keysn = keysn_ref[...].reshape(nb * t, PAGE)

    def count(pred):
        return (jnp.sum(jnp.where(pred(keys), 1.0, 0.0), axis=-1, keepdims=True)
                + jnp.sum(jnp.where(pred(keysn), 1.0, 0.0), axis=-1, keepdims=True))
    kth = _kth_largest(lambda cand: count(lambda x: x >= cand), float(n_sel), (nb * t, 1))
    need = float(n_sel) - count(lambda x: x > kth)
    tied = count(lambda x: x == kth)
    wide = lambda a: jnp.broadcast_to(a, (nb * t, LANE)).reshape(nb, t, LANE)
    kth_ref[...] = wide(kth)
    need_ref[...] = wide(need)
    tie_ref[...] = wide(jnp.where(tied > need, 1.0, 0.0))


def _threshold(keys, keysn, n_sel):
    b, t, past = keys.shape
    nb = min(8, b)
    blk = lambda w: pl.BlockSpec((nb, t, w), lambda i: (i, 0, 0))
    return pl.pallas_call(
        functools.partial(_threshold_kernel, n_sel=n_sel), grid=(b // nb,),
        in_specs=[blk(past), blk(PAGE)], out_specs=[blk(LANE)] * 3,
        out_shape=[jax.ShapeDtypeStruct((b, t, LANE), jnp.int32), jax.ShapeDtypeStruct((b, t, LANE), F32),
                   jax.ShapeDtypeStruct((b, t, LANE), F32)], name='topk_threshold',
        compiler_params=pltpu.CompilerParams(dimension_semantics=('arbitrary',), vmem_limit_bytes=VMEM_LIMIT),
    )(keys, keysn)


def _dsa_decode_kernel(pt_ref, tie_ref, keys_ref, keysn_ref, kth_ref, need_ref, bq_ref, kn_ref, vn_ref,
                       k_hbm, v_hbm, o_ref, carry_ref, bias_ref, m_ref, l_ref, acc_ref, k_buf, v_buf, sems,
                       *, npg, t):
    slot = _fetch_pages(pt_ref, (k_hbm, v_hbm), (k_buf, v_buf), sems, npg)
    kp = [k_buf.at[slot, pg] for pg in range(npg)]
    vp = [v_buf.at[slot, pg] for pg in range(npg)]
    bb, g = pl.program_id(0), pl.program_id(1)
    nk = npg * PAGE
    gsz = H_B // KVH_B

    @pl.when(g == 0)
    def _():
        carry_ref[...] = jnp.zeros(carry_ref.shape, F32)
        m_ref[...] = jnp.full(m_ref.shape, NEG, F32)
        l_ref[...] = jnp.zeros(l_ref.shape, F32)
        acc_ref[...] = jnp.zeros(acc_ref.shape, F32)

    q = bq_ref[0]
    kth = kth_ref[0][:, :1]
    need = need_ref[0][:, :1]
    ties_matter = tie_ref[bb] != 0

    def attend(keys_blk, extra, scores, values):
        w = keys_blk.shape[1]

        visible = True if extra is None else extra

        @pl.when(ties_matter)
        def _():
            sel, carry = _select(keys_blk, kth, need, carry_ref[...])
            carry_ref[...] = carry
            bias_ref[:, :w] = jnp.where(sel & visible, 0.0, NEG)

        @pl.when(jnp.logical_not(ties_matter))
        def _():
            bias_ref[:, :w] = jnp.where((keys_blk >= kth) & visible, 0.0, NEG)

        rs = [slice(n * gsz * t, (n + 1) * gsz * t) for n in range(KVH_B)]
        s = jnp.concatenate([scores(n, q[rs[n], :]) for n in range(KVH_B)], axis=0)
        s = s + jnp.concatenate([bias_ref[:, :w]] * H_B, axis=0)
        m_prev = m_ref[...]
        m_new = jnp.maximum(m_prev, jnp.max(s, axis=-1, keepdims=True))
        p = jnp.exp(s - m_new)
        alpha = jnp.exp(m_prev - m_new)
        l_ref[...] = alpha * l_ref[...] + jnp.sum(p, axis=-1, keepdims=True)
        pv = jnp.concatenate([values(n, p[rs[n], :].astype(BF16)) for n in range(KVH_B)], axis=0)
        acc_ref[...] = alpha * acc_ref[...] + pv
        m_ref[...] = m_new

    kt = [jnp.concatenate([kp[pg][n].astype(BF16) for pg in range(npg)], axis=1) for n in range(KVH_B)]
    vt = [jnp.concatenate([vp[pg][n].astype(BF16) for pg in range(npg)], axis=1) for n in range(KVH_B)]
    off = pl.multiple_of(g * nk, nk)
    attend(keys_ref[0, :, pl.ds(off, nk)], None,
           lambda n, qn: _dot(qn, kt[n]), lambda n, p16: _dot_nt(p16, vt[n]))

    @pl.when(g == pl.num_programs(1) - 1)
    def _():
        pad = lambda a: jnp.concatenate([a, jnp.zeros((PAGE - t, a.shape[1]), a.dtype)], axis=0)
        col = lax.broadcasted_iota(jnp.int32, (t, PAGE), 1)
        qt = lax.broadcasted_iota(jnp.int32, (t, PAGE), 0)
        k16, v16 = pad(kn_ref[0]).astype(BF16), pad(vn_ref[0]).astype(BF16)
        attend(keysn_ref[0], col <= qt,
               lambda n, qn: _dot_nt(qn, k16[:, n * HD_B:(n + 1) * HD_B]),
               lambda n, p16: _dot(p16, v16[:, n * HD_B:(n + 1) * HD_B]))
        o = acc_ref[...] / l_ref[...]
        o_ref[0] = jnp.concatenate([o[h * t:(h + 1) * t, :] for h in range(H_B)], axis=1).astype(BF16)


def _dsa_decode(pt_flat, keys, keysn, bqs, kn, vn, k_t, v_t, b, t, n_pages):
    npg = min(DECODE_PAGES, n_pages)
    ng = n_pages // npg
    past = n_pages * PAGE
    kth, need, tie = _threshold(keys, keysn, min(TOPK_MAX, (past + t) // 4))
    tie_flag = (jnp.max(tie, axis=(1, 2)) > 0.0).astype(jnp.int32)
    per_b = lambda a: pl.BlockSpec((1,) + a.shape[1:], lambda bb, g, pt, tf: (bb,) + (0,) * (a.ndim - 1))

    small = [keys, keysn, kth, need, bqs, kn, vn]
    grid_spec = pltpu.PrefetchScalarGridSpec(
        num_scalar_prefetch=2, grid=(b, ng),
        in_specs=[per_b(a) for a in small] + [pl.BlockSpec(memory_space=pl.ANY)] * 2,
        out_specs=pl.BlockSpec((1, t, 512), lambda bb, g, pt, tf: (bb, 0, 0)),
        scratch_shapes=[pltpu.VMEM((t, 1), F32), pltpu.VMEM((t, npg * PAGE), F32),
                        pltpu.VMEM((H_B * t, 1), F32), pltpu.VMEM((H_B * t, 1), F32),
                        pltpu.VMEM((H_B * t, HD_B), F32),
                        pltpu.VMEM((2, npg, KVH_B, HD_B, PAGE), F32), pltpu.VMEM((2, npg, KVH_B, HD_B, PAGE), F32),
                        pltpu.SemaphoreType.DMA((2, 2))])
    return pl.pallas_call(
        functools.partial(_dsa_decode_kernel, npg=npg, t=t), grid_spec=grid_spec,
        out_shape=jax.ShapeDtypeStruct((b, t, 512), BF16), name='dsa_decode',
        compiler_params=pltpu.CompilerParams(dimension_semantics=('arbitrary', 'arbitrary'),
                                             vmem_limit_bytes=VMEM_LIMIT),
    )(pt_flat, tie_flag, *small, k_t, v_t)


def _indicator(width, seg):
    e = np.zeros((width, LANE), np.float32)
    e[np.arange(width), np.arange(width) // seg] = 1.0
    return e


def _prep_layer(w, l):
    w_in_t = jnp.swapaxes(w['w_in'], 1, 2)[l]
    o_b, o_c, o_g = N_A, N_A + N_B, N_A + N_B + N_C
    w_uq = w['w_uq'][l].reshape(R_Q, H_A, DN_A + DR_A)
    wq_r = jnp.pad(w_uq[:, :, DN_A:], ((0, 0), (0, 0), (0, LANE - DR_A))).reshape(R_Q, H_A * LANE)
    g_q_a = w['g_q_a'][l]
    w_uk = w['w_uk'][l]
    eye = jnp.eye(H_A, dtype=F32)
    wabs = (eye[:, None, :, None] * jnp.transpose(w_uk, (1, 2, 0))[:, :, None, :]).reshape(H_A * DN_A, H_A * R_KV)
    wuv_bd = (eye[:, None, :, None] * jnp.transpose(w['w_uv'][l], (1, 0, 2))[:, :, None, :]
              ).reshape(H_A * R_KV, H_A * DV_A)
    e64 = _indicator(512, 64)
    e128 = _indicator(1024, 128)
    row = lambda a: a.reshape(1, -1)
    nq_b = H_B * HD_B
    padr = lambda a, n: jnp.pad(a, ((0, n - a.shape[0]), (0, 0))).astype(BF16)
    wbq = w_in_t[o_b:o_b + nq_b].reshape(H_B, HD_B, D_MODEL)[np.array(QHEAD_ORDER)].reshape(nq_b, D_MODEL)
    wb = jnp.concatenate([wbq, w_in_t[o_b + nq_b:o_c]], axis=0)
    return dict(
        g_attn=row(w['g_attn'][l]),
        wa=padr(w_in_t[:o_b], W_A), wb=padr(wb, W_B), wc=padr(w_in_t[o_c:o_g], W_C), wg=w_in_t[o_g:].astype(BF16),
        g_qa=row(w['g_qa'][l]),
        wq2=jnp.concatenate([w_uq[:, :, :DN_A].reshape(R_Q, H_A * DN_A), wq_r], axis=1).astype(BF16),
        gn=row(jnp.tile(g_q_a[:DN_A], H_A)),
        gr=row(jnp.tile(jnp.pad(g_q_a[DN_A:], (0, LANE - DR_A)), H_A)),
        g_kva=row(w['g_kva'][l]),
        wuk=w_uk.reshape(R_KV, H_A * DN_A).astype(BF16), wukt=w_uk.reshape(R_KV, H_A * DN_A).T.astype(BF16),
        wabs=(wabs * SCALE_A).astype(BF16), wuv_bd=wuv_bd.astype(BF16),
        gqb=row(jnp.tile(w['g_q_b'][l], H_B)), gkb=row(jnp.tile(w['g_k_b'][l], KVH_B)),
        e64=jnp.asarray(e64, BF16), e64t=jnp.asarray(e64.T, BF16),
        e128=jnp.asarray(e128, BF16), e128t=jnp.asarray(e128.T, BF16),
        e64x=jnp.asarray(np.kron(np.eye(H_A, dtype=np.float32), np.ones((DN_A, LANE), np.float32)), BF16),
        wa2=jnp.pad(w['w_a2'][l], ((0, LANE - R_GATE_C), (0, 0))).astype(BF16), b_a=row(w['b_a'][l]),
        goc=row(jnp.tile(w['g_o_c'][l], H_C)),
        w_pa=w['w_pa'][l].astype(BF16), w_pb=w['w_pb'][l].astype(BF16), w_pc=w['w_pc'][l].astype(BF16),
        w_o=w['w_o'][l].astype(BF16), g_ffn=row(w['g_ffn'][l]),
        w_up=w['w_up'][l].astype(BF16), w_down=w['w_down'][l].astype(BF16),
        g_ple=row(w['g_ple'][l]), w_pe=w['w_pe'][l].astype(BF16), w_pg=w['w_pg'][l].astype(BF16),
    )


def _rope_tables(pos):
    def tab(rot, period):
        inv = ROPE_THETA ** (-jnp.arange(0, rot, 2, dtype=F32) / rot)
        ang = pos.astype(F32)[:, None] * inv[None, :]
        c, s = jnp.cos(ang), jnp.sin(ang)
        npass = period - rot
        cos = jnp.concatenate([c, c, jnp.ones((pos.shape[0], npass), F32)], axis=1)
        sin = jnp.concatenate([-s, s, jnp.zeros((pos.shape[0], npass), F32)], axis=1)
        return jnp.tile(cos, (1, LANE // period)), jnp.tile(sin, (1, LANE // period))
    ca, sa = tab(DR_A, LANE)
    cb, sb = tab(ROT_B, HD_B)
    return ca, sa, cb, sb


def _head_major(a, b, t, nh, d):
    return a.reshape(b, t, nh, d).transpose(0, 2, 1, 3).reshape(b, nh * t, d)


def _gla_state_out(st, b):
    return st.reshape(b, DV_C, H_C, DK_C).transpose(0, 2, 3, 1)


def _layer(xp, xs, pp, ps, lw, l, caches, s_prev, pt_flat, tabs_p, tabs_s, dims):
    bp, tp, bs, ts, n_pages = dims
    cache_ckv, krope_t, k_t, v_t, kidx_t = caches
    np_, ns = bp * tp, bs * ts
    tm_p, tm_s = min(256, np_), min(256, ns)

    (q, kcat, c, kr, kinv, ct, bq, bk, bkb, bv, bvt, iq, ik, ikb, iwt) = _front_ab(xp, tabs_p, lw, tm_p)
    cq, ck, cv, la, cr, gates = _front_cg(xp, lw, tm_p)
    oa = _mla_prompt(q, kcat, ct, kinv, lw['wuv_bd'], bp, tp)
    ob = _dsa_prompt(bq, iq, iwt, ikb, bkb, bvt, bp, tp)
    oc, st = _gla(cq, ck, cv, la, jnp.zeros((bp, DV_C, H_C * DK_C), F32), bp, tp)
    x1 = _back_merge(xp, oa, ob, oc, cr, gates, lw, tm_p)
    yp = _back_ffn(x1, pp, lw, min(512, np_))
    rows_p = (c.reshape(bp, tp, R_KV), kr.reshape(bp, tp, DR_A), bk.reshape(bp, tp, KVH_B, HD_B),
              bv.reshape(bp, tp, KVH_B, HD_B), ik.reshape(bp, tp, DI_B), _gla_state_out(st, bp))

    (q, kcat, c, kr, kinv, ct, bq, bk, bkb, bv, bvt, iq, ik, ikb, iwt) = _front_ab(xs, tabs_s, lw, tm_s)
    cq, ck, cv, la, cr, gates = _front_cg(xs, lw, tm_s)
    qs = _head_major(q, bs, ts, H_A, 256)
    iqs = _head_major(iq, bs, ts, HI_B, DI_B)
    iws = jnp.broadcast_to(iwt[:HI_B].reshape(HI_B, bs, ts).transpose(1, 0, 2).reshape(bs, HI_B * ts, 1),
                           (bs, HI_B * ts, LANE))
    kinvn = jnp.pad(kinv[:, ::LANE].reshape(bs, ts, H_A).transpose(0, 2, 1), ((0, 0), (0, 0), (0, PAGE - ts)))
    oa, keys, keysn = _mla_decode(pt_flat, qs, iqs, iws, c.reshape(bs, ts, R_KV), kr.reshape(bs, ts, DR_A),
                                  kinvn, ik.reshape(bs, ts, DI_B), lw, cache_ckv, krope_t, kidx_t,
                                  bs, ts, n_pages)
    std_order = [QHEAD_ORDER.index(h) for h in range(H_B)]
    bqs = bq.reshape(bs, ts, H_B, HD_B)[:, :, std_order, :].transpose(0, 2, 1, 3).reshape(bs, H_B * ts, HD_B)
    ob = _dsa_decode(pt_flat, keys, keysn, bqs, bk.reshape(bs, ts, 128), bv.reshape(bs, ts, 128),
                     k_t, v_t, bs, ts, n_pages)
    s0t = s_prev.transpose(0, 3, 1, 2).reshape(bs, DV_C, H_C * DK_C)
    oc, st = _gla(cq, ck, cv, la, s0t, bs, ts)
    x1 = _back_merge(xs, oa.reshape(ns, 512), ob.reshape(ns, 512), oc, cr, gates, lw, tm_s)
    ys = _back_ffn(x1, ps, lw, min(512, ns))
    rows_s = (c.reshape(bs, ts, R_KV), kr.reshape(bs, ts, DR_A), bk.reshape(bs, ts, KVH_B, HD_B),
              bv.reshape(bs, ts, KVH_B, HD_B), ik.reshape(bs, ts, DI_B), _gla_state_out(st, bs))
    return yp, ys, rows_p, rows_s


def kernel(x_prompt, x_sample, cache_ckv, cache_krope, cache_k, cache_v, cache_kidx, state_gla, page_table,
           p_prompt, p_sample, g_attn, w_in, g_qa, w_uq, g_q_a, g_kva, w_uk, w_uv, g_q_b, g_k_b, w_a2, b_a,
           g_o_c, w_pa, w_pb, w_pc, w_o, g_ffn, w_up, w_down, g_ple, w_pe, w_pg):
    w = dict(g_attn=g_attn, w_in=w_in, g_qa=g_qa, w_uq=w_uq, g_q_a=g_q_a, g_kva=g_kva, w_uk=w_uk, w_uv=w_uv,
             g_q_b=g_q_b, g_k_b=g_k_b, w_a2=w_a2, b_a=b_a, g_o_c=g_o_c, w_pa=w_pa, w_pb=w_pb, w_pc=w_pc,
             w_o=w_o, g_ffn=g_ffn, w_up=w_up, w_down=w_down, g_ple=g_ple, w_pe=w_pe, w_pg=w_pg)
    depth = w_in.shape[0]
    bp, tp, _ = x_prompt.shape
    bs, ts, _ = x_sample.shape
    n_pages = page_table.shape[1]
    past = n_pages * PAGE
    np_, ns = bp * tp, bs * ts
    tabs_p = _rope_tables(jnp.arange(tp))
    tm_s = min(256, ns)
    tabs_s = _rope_tables(past + (jnp.arange(tm_s) % ts))
    pt_flat = page_table.reshape(-1)
    xp, xs = x_prompt.reshape(np_, D_MODEL), x_sample.reshape(ns, D_MODEL)
    pages = lambda a: a.reshape((-1,) + a.shape[2:])
    caches = (pages(cache_ckv), pages(jnp.swapaxes(cache_krope, 2, 3)),
              pages(jnp.transpose(cache_k, (0, 1, 3, 4, 2))), pages(jnp.transpose(cache_v, (0, 1, 3, 4, 2))),
              pages(jnp.swapaxes(cache_kidx, 2, 3)))
    n_phys = cache_ckv.shape[1]
    new_p, new_s = [], []
    for l in range(depth):
        lw = _prep_layer(w, l)
        xp, xs, rows_p, rows_s = _layer(
            xp, xs, p_prompt[l].reshape(np_, D_PLE), p_sample[l].reshape(ns, D_PLE), lw, l,
            caches, state_gla[l], pt_flat + l * n_phys, tabs_p, tabs_s,
            (bp, tp, bs, ts, n_pages))
        new_p.append(rows_p)
        new_s.append(rows_s)
    stack = lambda rows, i: jnp.stack([r[i] for r in rows], axis=0)
    return ((xp.reshape(bp, tp, D_MODEL), xs.reshape(bs, ts, D_MODEL))
            + tuple(stack(new_p, i) for i in range(6)) + tuple(stack(new_s, i) for i in range(6)))
```

```python
import functools

import numpy as np
import jax
import jax.numpy as jnp
from jax import lax
from jax.experimental import pallas as pl
from jax.experimental.pallas import tpu as pltpu

D_MODEL = 1024
PAGE = 128
H_A, DN_A, DR_A, DV_A, R_Q, R_KV = 8, 64, 32, 64, 256, 128
SCALE_A = (DN_A + DR_A) ** -0.5
H_B, KVH_B, HD_B = 8, 2, 64
ROT_B = HD_B // 4
HI_B, DI_B = 4, 64
TOPK_MAX = 256
H_C, DK_C, DV_C, R_GATE_C = 4, 64, 128, 16
GLA_TAU = 16.0
GLA_BLOCK = 64
DECODE_PAGES = 64
ONES_ROWS = 16
V_ROWS = HD_B + ONES_ROWS
C_ROWS = R_KV + ONES_ROWS
QHEAD_ORDER = (0, 4, 1, 5, 2, 6, 3, 7)
D_FF = 4 * D_MODEL
D_PLE = 256
ROPE_THETA = 500000.0
EPS = 1e-6

N_A = R_Q + R_KV + DR_A
N_B = H_B * HD_B + 2 * KVH_B * HD_B + HI_B * DI_B + DI_B + HI_B
N_C = 2 * H_C * DK_C + 2 * H_C * DV_C + R_GATE_C
W_A, W_B, W_C = 512, 1152, 1664
LANE = 128
NEG = -1e30
INT_MIN = -(2 ** 31)

F32 = jnp.float32
BF16 = jnp.bfloat16
VMEM_LIMIT = 48 * 1024 * 1024


def _dot(a, b):
    return jnp.dot(a, b, preferred_element_type=F32)


def _dot_nt(a, b):
    return lax.dot_general(a, b, (((1,), (1,)), ((), ())), preferred_element_type=F32)


def _dot_tn(a, b):
    return lax.dot_general(a, b, (((0,), (0,)), ((), ())), preferred_element_type=F32)


def _split3(a):
    a1 = a.astype(BF16)
    r1 = a - a1.astype(F32)
    a2 = r1.astype(BF16)
    a3 = (r1 - a2.astype(F32)).astype(BF16)
    return a1, a2, a3


def _dot_f32(a, b_bf):
    a1, a2, a3 = _split3(a)
    return _dot(a1, b_bf) + _dot(a2, b_bf) + _dot(a3, b_bf)


def _rms(x, g):
    return x * lax.rsqrt(jnp.mean(x * x, axis=-1, keepdims=True) + EPS) * g


def _seg_inv(x, e, et, n):
    inv = lax.rsqrt(_dot_f32(x * x, e) * (1.0 / n) + EPS)
    return _dot_f32(inv, et)


def _rope(x, cos, sin, period, half):
    w = x.shape[-1]
    reps = w // LANE
    if reps > 1:
        cos = jnp.concatenate([cos] * reps, axis=1)
        sin = jnp.concatenate([sin] * reps, axis=1)
    lane = lax.broadcasted_iota(jnp.int32, x.shape, 1)
    first = (lane % period) < half
    partner = jnp.where(first, pltpu.roll(x, w - half, 1), pltpu.roll(x, half, 1))
    return x * cos + partner * sin


def _sigmoid(x):
    return 1.0 / (1.0 + jnp.exp(-x))


def _front_ab_kernel(x_ref, g_ref, wa_ref, wb_ref, gqa_ref, wq_ref, gn_ref, gr_ref, gkv_ref, wuk_ref,
                     wabs_ref, gqb_ref, gkb_ref, ca_ref, sa_ref, cb_ref, sb_ref,
                     e64_ref, e64t_ref, e128_ref, e128t_ref, e64x_ref,
                     q_ref, kcat_ref, c_ref, kr_ref, kinv_ref, ct_ref,
                     bq_ref, bk_ref, bkb_ref, bv_ref, bvt_ref, iq_ref, ik_ref, ikb_ref, iwt_ref):
    xn = _rms(x_ref[...], g_ref[...]).astype(BF16)
    za = _dot_nt(xn, wa_ref[...])
    zb = _dot_nt(xn, wb_ref[...])
    ca, sa, cb, sb = ca_ref[...], sa_ref[...], cb_ref[...], sb_ref[...]
    e64, e64t = e64_ref[...], e64t_ref[...]

    aq = _rms(za[:, :R_Q], gqa_ref[...]).astype(BF16)
    q2 = _dot(aq, wq_ref[...])
    qn, qr = q2[:, :512], q2[:, 512:]
    ss = _dot_f32(qn * qn, e64) + _dot_f32(qr * qr, e128_ref[...])
    inv = lax.rsqrt(ss * (1.0 / (DN_A + DR_A)) + EPS)
    qn = qn * _dot_f32(inv, e64t) * gn_ref[...]
    qr = qr * _dot_f32(inv, e128t_ref[...]) * gr_ref[...]
    qr = _rope(qr, ca, sa, LANE, DR_A // 2) * SCALE_A
    qlat = _dot(qn.astype(BF16), wabs_ref[...])
    for h in range(H_A):
        q_ref[:, h * 256:h * 256 + 128] = qlat[:, h * 128:(h + 1) * 128].astype(BF16)
        q_ref[:, h * 256 + 128:(h + 1) * 256] = qr[:, h * 128:(h + 1) * 128].astype(BF16)

    c = _rms(za[:, R_Q:R_Q + R_KV], gkv_ref[...])
    kr = _rope(za[:, R_Q + R_KV:], ca, sa, LANE, DR_A // 2)
    c_ref[...] = c
    kr_ref[...] = kr[:, :DR_A]
    cb16 = c.astype(BF16)
    kcat_ref[:, :R_KV] = cb16
    kcat_ref[:, R_KV:] = kr.astype(BF16)
    ones = jnp.ones((ONES_ROWS, c.shape[0]), F32)
    ct_ref[...] = jnp.concatenate([c.T, ones], axis=0).astype(BF16)
    kn = _dot(cb16, wuk_ref[...])
    ms = _dot_f32(kn * kn, e64x_ref[...]) + jnp.sum(kr * kr, axis=-1, keepdims=True)
    kinv_ref[...] = lax.rsqrt(ms * (1.0 / (DN_A + DR_A)) + EPS)

    bq = zb[:, :512]
    bq = bq * _seg_inv(bq, e64, e64t, HD_B) * gqb_ref[...]
    bq_ref[...] = (_rope(bq, cb, sb, HD_B, ROT_B // 2) * (HD_B ** -0.5)).astype(BF16)
    bk = zb[:, 512:640]
    bk = bk * _seg_inv(bk, e64[:128], e64t[:, :128], HD_B) * gkb_ref[...]
    bk = _rope(bk, cb, sb, HD_B, ROT_B // 2)
    bk_ref[...] = bk
    bkb_ref[...] = bk.astype(BF16)
    bv = zb[:, 640:768]
    bv_ref[...] = bv
    bvt = bv.T
    bvt_ref[...] = jnp.concatenate([bvt[:HD_B], ones, bvt[HD_B:], ones], axis=0).astype(BF16)
    iq_ref[...] = (_rope(zb[:, 768:1024], cb, sb, DI_B, ROT_B // 2) * (DI_B ** -0.5)).astype(BF16)
    last = zb[:, 1024:1152]
    ik = _rope(last, cb, sb, DI_B, ROT_B // 2)[:, :DI_B]
    ik_ref[...] = ik
    ikb_ref[...] = ik.astype(BF16)
    iwt_ref[...] = (last * (HI_B ** -0.5)).T[DI_B:DI_B + 8, :]


def _front_ab(x2d, tabs, lw, tm):
    n = x2d.shape[0]
    nblk = tabs[0].shape[0] // tm
    row = lambda w: pl.BlockSpec((tm, w), lambda i: (i, 0))
    full = lambda a: pl.BlockSpec(a.shape, lambda i: (0,) * a.ndim)
    tab = pl.BlockSpec((tm, LANE), lambda i: (i % nblk, 0))
    consts = [lw['g_attn'], lw['wa'], lw['wb'], lw['g_qa'], lw['wq2'], lw['gn'], lw['gr'], lw['g_kva'],
              lw['wuk'], lw['wabs'], lw['gqb'], lw['gkb']]
    inds = [lw['e64'], lw['e64t'], lw['e128'], lw['e128t'], lw['e64x']]
    outs = [(2048, BF16, None), (256, BF16, None), (R_KV, F32, None), (DR_A, F32, None),
            (H_A * LANE, F32, None), (C_ROWS, BF16),
            (512, BF16, None), (128, F32, None), (128, BF16, None), (128, F32, None), (KVH_B * V_ROWS, BF16),
            (256, BF16, None), (DI_B, F32, None), (DI_B, BF16, None), (8, F32)]
    out_shape, out_specs = [], []
    for o in outs:
        if len(o) == 2:
            out_shape.append(jax.ShapeDtypeStruct((o[0], n), o[1]))
            out_specs.append(pl.BlockSpec((o[0], tm), lambda i: (0, i)))
        else:
            out_shape.append(jax.ShapeDtypeStruct((n, o[0]), o[1]))
            out_specs.append(row(o[0]))
    return pl.pallas_call(
        _front_ab_kernel, grid=(n // tm,),
        in_specs=[row(D_MODEL)] + [full(a) for a in consts] + [tab] * 4 + [full(a) for a in inds],
        out_specs=out_specs, out_shape=out_shape, name='front_ab',
        compiler_params=pltpu.CompilerParams(dimension_semantics=('arbitrary',), vmem_limit_bytes=VMEM_LIMIT),
    )(x2d, *consts, *tabs, *inds)


def _front_cg_kernel(x_ref, g_ref, wc_ref, wg_ref, wa2_ref, ba_ref,
                     cq_ref, ck_ref, cv_ref, la_ref, cr_ref, gate_ref):
    xn = _rms(x_ref[...], g_ref[...]).astype(BF16)
    zc = _dot_nt(xn, wc_ref[...])
    cq_ref[...] = zc[:, :256]
    ck_ref[...] = zc[:, 256:512] * (DK_C ** -0.5)
    cv_ref[...] = zc[:, 512:1024]
    cr_ref[...] = zc[:, 1024:1536]
    u = _dot(zc[:, 1536:1664].astype(BF16), wa2_ref[...]) + ba_ref[...]
    la_ref[...] = (jnp.minimum(u, 0.0) - jnp.log1p(jnp.exp(-jnp.abs(u)))) * (1.0 / GLA_TAU)
    gate_ref[...] = _dot_nt(xn, wg_ref[...])


def _front_cg(x2d, lw, tm):
    n = x2d.shape[0]
    row = lambda w: pl.BlockSpec((tm, w), lambda i: (i, 0))
    full = lambda a: pl.BlockSpec(a.shape, lambda i: (0,) * a.ndim)
    consts = [lw['g_attn'], lw['wc'], lw['wg'], lw['wa2'], lw['b_a']]
    widths = [256, 256, 512, 256, 512, 3 * D_MODEL]
    return pl.pallas_call(
        _front_cg_kernel, grid=(n // tm,),
        in_specs=[row(D_MODEL)] + [full(a) for a in consts],
        out_specs=[row(w) for w in widths],
        out_shape=[jax.ShapeDtypeStruct((n, w), F32) for w in widths], name='front_cg',
        compiler_params=pltpu.CompilerParams(dimension_semantics=('arbitrary',), vmem_limit_bytes=VMEM_LIMIT),
    )(x2d, *consts)


def _mla_prompt_kernel(qi_ref, kj_ref, q_ref, k_ref, ct_ref, kinv_ref, wuv_ref, o_ref, m_ref, acc_ref, *, tq):
    i, j = qi_ref[pl.program_id(1)], kj_ref[pl.program_id(1)]

    @pl.when(j == 0)
    def _():
        m_ref[...] = jnp.full(m_ref.shape, NEG, F32)
        acc_ref[...] = jnp.zeros(acc_ref.shape, F32)

    def step(diagonal):
        k = k_ref[...]
        ct = ct_ref[...]
        if diagonal:
            visible = (lax.broadcasted_iota(jnp.int32, (tq, tq), 0) <= lax.broadcasted_iota(jnp.int32, (tq, tq), 1))
        def logits(h):
            kinv = kinv_ref[:, h * LANE:(h + 1) * LANE]
            s = _dot_nt(k, q_ref[:, h * 256:(h + 1) * 256]) * jnp.concatenate([kinv] * (tq // LANE), axis=1)
            return jnp.where(visible, s, NEG) if diagonal else s

        s_next = logits(0)
        for h in range(H_A):
            s = s_next
            if h + 1 < H_A:
                s_next = logits(h + 1)
            m_prev = m_ref[h:h + 1, :]
            m_new = jnp.maximum(m_prev, jnp.max(s, axis=0, keepdims=True))
            p = jnp.exp(s - m_new).astype(BF16)
            acc_ref[h] = jnp.exp(m_prev - m_new) * acc_ref[h] + _dot(ct, p)
            m_ref[h:h + 1, :] = m_new

    @pl.when(j < i)
    def _():
        step(False)

    @pl.when(j == i)
    def _():
        step(True)
        olat_t = jnp.concatenate([acc_ref[h, :R_KV, :] / acc_ref[h, R_KV:R_KV + 1, :] for h in range(H_A)],
                                 axis=0)
        o_ref[...] = _dot(olat_t.T.astype(BF16), wuv_ref[...]).astype(BF16)


def _mla_prompt(q, kcat, ct, kinv, wuv_bd, b, t):
    tq = min(256, t)
    nq = t // tq
    pairs = [(i, j) for i in range(nq) for j in range(i + 1)]
    qi = jnp.asarray([p[0] for p in pairs], jnp.int32)
    kj = jnp.asarray([p[1] for p in pairs], jnp.int32)
    qrow = lambda bb, p, qi, kj: (bb * nq + qi[p], 0)
    krow = lambda bb, p, qi, kj: (bb * nq + kj[p], 0)
    grid_spec = pltpu.PrefetchScalarGridSpec(
        num_scalar_prefetch=2, grid=(b, len(pairs)),
        in_specs=[pl.BlockSpec((tq, 2048), qrow), pl.BlockSpec((tq, 256), krow),
                  pl.BlockSpec((C_ROWS, tq), lambda bb, p, qi, kj: (0, bb * nq + kj[p])),
                  pl.BlockSpec((tq, H_A * LANE), krow),
                  pl.BlockSpec(wuv_bd.shape, lambda bb, p, qi, kj: (0, 0))],
        out_specs=pl.BlockSpec((tq, 512), qrow),
        scratch_shapes=[pltpu.VMEM((H_A, tq), F32), pltpu.VMEM((H_A, C_ROWS, tq), F32)])
    return pl.pallas_call(
        functools.partial(_mla_prompt_kernel, tq=tq), grid_spec=grid_spec,
        out_shape=jax.ShapeDtypeStruct((b * t, 512), BF16), name='mla_prompt',
        compiler_params=pltpu.CompilerParams(dimension_semantics=('arbitrary', 'arbitrary'),
                                             vmem_limit_bytes=VMEM_LIMIT),
    )(qi, kj, q, kcat, ct, kinv, wuv_bd)


def _score_key(sc):
    sc = jnp.where(sc == 0.0, 0.0, sc)
    bits = lax.bitcast_convert_type(sc, jnp.int32)
    return jnp.where(bits < 0, bits ^ 0x7FFFFFFF, bits)


def _kth_largest(count_ge, k, shape):
    base = jnp.where(count_ge(jnp.zeros(shape, jnp.int32)) >= k, 0, INT_MIN).astype(jnp.int32)

    def body(it, base):
        cand = base + jnp.left_shift(jnp.int32(1), 30 - it)
        return jnp.where(count_ge(cand) >= k, cand, base)
    return lax.fori_loop(0, 31, body, base)


def _cumsum_lanes(x):
    r, w = x.shape
    nb = w // LANE
    xs = jnp.concatenate([x[:, j * LANE:(j + 1) * LANE] for j in range(nb)], axis=0) if nb > 1 else x
    a = lax.broadcasted_iota(jnp.int32, (LANE, LANE), 0)
    b = lax.broadcasted_iota(jnp.int32, (LANE, LANE), 1)
    p = _dot(xs.astype(BF16), jnp.where(a <= b, 1.0, 0.0).astype(BF16))
    if nb == 1:
        return p
    ra = lax.broadcasted_iota(jnp.int32, (nb * r, nb * r), 0)
    rb = lax.broadcasted_iota(jnp.int32, (nb * r, nb * r), 1)
    earlier = jnp.where((rb // r < ra // r) & (rb % r == ra % r), 1.0, 0.0).astype(BF16)
    tot = jnp.broadcast_to(p[:, LANE - 1:LANE], (nb * r, LANE)).astype(BF16)
    p = p + _dot(earlier, tot)
    return jnp.concatenate([p[j * r:(j + 1) * r, :] for j in range(nb)], axis=1)


def _select(keys, kth, need, carry):
    eq = (keys == kth).astype(F32)
    pref = _cumsum_lanes(eq) - eq + carry
    sel = (keys > kth) | ((eq > 0.0) & (pref < need))
    return sel, carry + jnp.sum(eq, axis=-1, keepdims=True)


def _dsa_prompt_kernel(bq_ref, iq_ref, iwt_ref, ik_ref, k_ref, vt_ref, o_ref, keys_ref, bias_ref,
                       *, tq, tkb, nkb, tile0, n_sel):
    i = pl.program_id(1)
    blocks = [slice(kb * tkb, (kb + 1) * tkb) for kb in range(nkb)]
    keyp = (nkb - 1) * tkb + lax.broadcasted_iota(jnp.int32, (tkb, tq), 0)
    visible = keyp <= (tile0 + i) * tq + lax.broadcasted_iota(jnp.int32, (tkb, tq), 1)
    iwt = iwt_ref[...]
    iq = [iq_ref[:, h * DI_B:(h + 1) * DI_B] for h in range(HI_B)]

    for kb, blk in enumerate(blocks):
        ik = ik_ref[blk, :]
        sc = jnp.zeros((tkb, tq), F32)
        for h in range(HI_B):
            sc = sc + jnp.maximum(_dot_nt(ik, iq[h]), 0.0) * iwt[h:h + 1, :]
        if kb == nkb - 1:
            sc = jnp.where(visible, sc, -jnp.inf)
        keys_ref[blk, :] = _score_key(sc)

    def count(pred):
        acc = jnp.zeros((1, tq), F32)
        for blk in blocks:
            acc = acc + jnp.sum(jnp.where(pred(keys_ref[blk, :]), 1.0, 0.0), axis=0, keepdims=True)
        return acc

    kth = _kth_largest(lambda cand: count(lambda x: x >= cand), float(n_sel), (1, tq))
    need = float(n_sel) - count(lambda x: x > kth)

    lower = jnp.where(lax.broadcasted_iota(jnp.int32, (tkb, tkb), 1) < lax.broadcasted_iota(jnp.int32, (tkb, tkb), 0),
                      1.0, 0.0).astype(BF16)
    ties = jnp.zeros((1, tq), F32)
    for kb, blk in enumerate(blocks):
        keys = keys_ref[blk, :]
        eq = keys == kth
        eqf = jnp.where(eq, 1.0, 0.0)
        before = _dot(lower, eqf.astype(BF16)) + ties
        sel = (keys > kth) | (eq & (before < need))
        if kb == nkb - 1:
            sel = sel & visible
        bias_ref[blk, :] = jnp.where(sel, 0.0, NEG)
        ties = ties + jnp.sum(eqf, axis=0, keepdims=True)

    upper_half = lax.broadcasted_iota(jnp.int32, (tq, LANE), 1) >= HD_B

    def logits(pos):
        qb = bq_ref[:, (pos // 2) * LANE:(pos // 2 + 1) * LANE]
        qpad = jnp.where(upper_half == (pos % 2 == 1), qb, jnp.zeros_like(qb))
        return [_dot_nt(k_ref[blk, :], qpad) + bias_ref[blk, :] for blk in blocks]

    outs = [None] * H_B
    s_next = logits(0)
    for pos in range(H_B):
        s = s_next
        if pos + 1 < H_B:
            s_next = logits(pos + 1)
        n = pos % 2
        m = functools.reduce(jnp.maximum, [jnp.max(sb, axis=0, keepdims=True) for sb in s])
        acc = jnp.zeros((V_ROWS, tq), F32)
        for sb, blk in zip(s, blocks):
            acc = acc + _dot(vt_ref[n * V_ROWS:(n + 1) * V_ROWS, blk], jnp.exp(sb - m).astype(BF16))
        outs[QHEAD_ORDER[pos]] = acc[:HD_B, :] / acc[HD_B:HD_B + 1, :]
    o_ref[0] = jnp.concatenate(outs, axis=0).T.astype(BF16)


def _dsa_prompt(bq, iq, iwt, ikb, bkb, bvt, b, t):
    tq = min(512, t)
    tkb = min(512, t)
    nq = t // tq
    tpc = tkb // tq
    n_sel = min(TOPK_MAX, t // 4)
    seq = lambda w: pl.BlockSpec((t, w), lambda bb, i: (bb, 0))
    outs = []
    for cls in range(t // tkb):
        nkb = cls + 1
        qrow = lambda w, cls=cls: pl.BlockSpec((tq, w), lambda bb, i: (bb * nq + cls * tpc + i, 0))
        outs.append(pl.pallas_call(
            functools.partial(_dsa_prompt_kernel, tq=tq, tkb=tkb, nkb=nkb, tile0=cls * tpc, n_sel=n_sel),
            grid=(b, tpc),
            in_specs=[qrow(512), qrow(256),
                      pl.BlockSpec((8, tq), lambda bb, i, cls=cls: (0, bb * nq + cls * tpc + i)),
                      seq(DI_B), seq(128), pl.BlockSpec((KVH_B * V_ROWS, t), lambda bb, i: (0, bb))],
            out_specs=pl.BlockSpec((1, tq, 512), lambda bb, i: (bb, i, 0)),
            out_shape=jax.ShapeDtypeStruct((b, tkb, 512), BF16),
            scratch_shapes=[pltpu.VMEM((nkb * tkb, tq), jnp.int32), pltpu.VMEM((nkb * tkb, tq), F32)],
            name='dsa_prompt',
            compiler_params=pltpu.CompilerParams(dimension_semantics=('arbitrary', 'arbitrary'),
                                                 vmem_limit_bytes=VMEM_LIMIT),
        )(bq, iq, iwt, ikb, bkb, bvt))
    return jnp.concatenate(outs, axis=1).reshape(b * t, 512)


def _gla_kernel(q_ref, k_ref, v_ref, la_ref, s0_ref, o_ref, sout_ref, st_ref, *, c, nsub):
    j = pl.program_id(1)

    @pl.when(j == 0)
    def _():
        st_ref[...] = s0_ref[0]

    kw = H_C * DK_C
    r = lax.broadcasted_iota(jnp.int32, (c, c), 0)
    cc = lax.broadcasted_iota(jnp.int32, (c, c), 1)
    tril = jnp.where(r >= cc, 1.0, 0.0).astype(BF16)
    row = lax.broadcasted_iota(jnp.int32, (c, kw), 0)
    head = lax.broadcasted_iota(jnp.int32, (c, kw), 1) // DK_C

    def heads(a):
        return [jnp.where(head == h, a, 0.0).astype(BF16) for h in range(H_C)]

    def body(i, carry):
        off = pl.multiple_of(i * c, c)
        la = la_ref[0, pl.ds(off, c), :]
        l1, l2, l3 = _split3(la)
        lb = _dot(tril, l1) + _dot(tril, l2) + _dot(tril, l3)
        q = q_ref[0, pl.ds(off, c), :]
        k = k_ref[0, pl.ds(off, c), :]
        v = v_ref[0, pl.ds(off, c), :]

        k16 = k.astype(BF16)
        att = [jnp.where(r == cc, _dot_nt(qh, k16), 0.0) for qh in heads(q)]
        fl = lb
        b = 1
        while b < c:
            up = (row % (2 * b)) >= b
            if b > 1:
                fl = jnp.where((row % b) >= b // 2, pltpu.roll(fl, b // 2, 0), fl)
            qf = jnp.where(up, q * jnp.exp(jnp.where(up, lb - fl, 0.0)), 0.0)
            nxt = pltpu.roll(fl, c - b, 0)
            kf = jnp.where(up, 0.0, k * jnp.exp(jnp.where(up, 0.0, nxt - lb))).astype(BF16)
            same = (r // (2 * b)) == (cc // (2 * b))
            att = [a + jnp.where(same, _dot_nt(qh, kf), 0.0) for a, qh in zip(att, heads(qf))]
            b *= 2

        st = st_ref[...]
        st16 = st.astype(BF16)
        lb_end = lb[c - 1:c, :]
        upd = jnp.zeros(st.shape, F32)
        for h, (qe, kt) in enumerate(zip(heads(q * jnp.exp(lb)), heads(k * jnp.exp(lb_end - lb)))):
            vh = v[:, h * DV_C:(h + 1) * DV_C].astype(BF16)
            o_ref[0, pl.ds(off, c), h * DV_C:(h + 1) * DV_C] = _dot(att[h].astype(BF16), vh) + _dot_nt(qe, st16)
            upd = upd + _dot_tn(vh, kt)
        st_ref[...] = st * jnp.exp(lb_end) + upd
        return carry
    lax.fori_loop(0, nsub, body, 0)

    @pl.when(j == pl.num_programs(1) - 1)
    def _():
        sout_ref[0] = st_ref[...]


def _gla(cq, ck, cv, la, s0t, b, t):
    c = min(GLA_BLOCK, t)
    tt = min(512, t)
    r3 = lambda a: a.reshape(b, t, a.shape[-1])
    tok = lambda w: pl.BlockSpec((1, tt, w), lambda bb, j: (bb, j, 0))
    st = pl.BlockSpec((1, DV_C, H_C * DK_C), lambda bb, j: (bb, 0, 0))
    oc, s_out = pl.pallas_call(
        functools.partial(_gla_kernel, c=c, nsub=tt // c), grid=(b, t // tt),
        in_specs=[tok(256), tok(256), tok(512), tok(256), st],
        out_specs=[tok(512), st],
        out_shape=[jax.ShapeDtypeStruct((b, t, 512), F32), jax.ShapeDtypeStruct((b, DV_C, H_C * DK_C), F32)],
        scratch_shapes=[pltpu.VMEM((DV_C, H_C * DK_C), F32)], name='gla',
        compiler_params=pltpu.CompilerParams(dimension_semantics=('arbitrary', 'arbitrary'),
                                             vmem_limit_bytes=VMEM_LIMIT),
    )(r3(cq), r3(ck), r3(cv), r3(la), s0t)
    return oc.reshape(b * t, 512), s_out


def _back_merge_kernel(x_ref, oa_ref, ob_ref, oc_ref, cr_ref, gate_ref, goc_ref,
                       wpa_ref, wpb_ref, wpc_ref, wo_ref, y_ref):
    oc, cr = oc_ref[...], cr_ref[...]
    parts = []
    for h in range(H_C):
        och = oc[:, h * DV_C:(h + 1) * DV_C]
        parts.append(och * lax.rsqrt(jnp.mean(och * och, axis=-1, keepdims=True) + EPS))
    ocn = jnp.concatenate(parts, axis=1) * goc_ref[...] * (cr * _sigmoid(cr))
    gate = gate_ref[...]
    merged = (_sigmoid(gate[:, :D_MODEL]) * _dot(oa_ref[...], wpa_ref[...])
              + _sigmoid(gate[:, D_MODEL:2 * D_MODEL]) * _dot(ob_ref[...], wpb_ref[...])
              + _sigmoid(gate[:, 2 * D_MODEL:]) * _dot(ocn.astype(BF16), wpc_ref[...]))
    y_ref[...] = x_ref[...] + _dot(merged.astype(BF16), wo_ref[...])


def _back_merge(x2d, oa, ob, oc, cr, gates, lw, tm):
    n = x2d.shape[0]
    row = lambda w: pl.BlockSpec((tm, w), lambda i: (i, 0))
    full = lambda a: pl.BlockSpec(a.shape, lambda i: (0,) * a.ndim)
    consts = [lw['goc'], lw['w_pa'], lw['w_pb'], lw['w_pc'], lw['w_o']]
    return pl.pallas_call(
        _back_merge_kernel, grid=(n // tm,),
        in_specs=[row(D_MODEL), row(512), row(512), row(512), row(512), row(3 * D_MODEL)]
        + [full(a) for a in consts],
        out_specs=row(D_MODEL), out_shape=jax.ShapeDtypeStruct((n, D_MODEL), F32), name='back_merge',
        compiler_params=pltpu.CompilerParams(dimension_semantics=('arbitrary',), vmem_limit_bytes=VMEM_LIMIT),
    )(x2d, oa, ob, oc, cr, gates, *consts)


def _back_ffn_kernel(x_ref, p_ref, gf_ref, wup_ref, wdn_ref, gp_ref, wpg_ref, wpe_ref, y_ref, xn_ref, acc_ref):
    f = pl.program_id(1)

    @pl.when(f == 0)
    def _():
        xn_ref[...] = _rms(x_ref[...], gf_ref[...]).astype(BF16)
        acc_ref[...] = x_ref[...]

    hid = jnp.maximum(_dot(xn_ref[...], wup_ref[...]), 0.0)
    acc_ref[...] += _dot((hid * hid).astype(BF16), wdn_ref[...])

    @pl.when(f == pl.num_programs(1) - 1)
    def _():
        x2 = acc_ref[...]
        gate = _sigmoid(_dot(_rms(x2, gp_ref[...]).astype(BF16), wpg_ref[...]))
        y_ref[...] = x2 + gate * _dot(p_ref[...].astype(BF16), wpe_ref[...])


def _back_ffn(x2d, p2d, lw, tm):
    n = x2d.shape[0]
    tf = 1024
    row = lambda w: pl.BlockSpec((tm, w), lambda i, f: (i, 0))
    full = lambda a: pl.BlockSpec(a.shape, lambda i, f: (0,) * a.ndim)
    return pl.pallas_call(
        _back_ffn_kernel, grid=(n // tm, D_FF // tf),
        in_specs=[row(D_MODEL), row(D_PLE), full(lw['g_ffn']),
                  pl.BlockSpec((D_MODEL, tf), lambda i, f: (0, f)), pl.BlockSpec((tf, D_MODEL), lambda i, f: (f, 0)),
                  full(lw['g_ple']), full(lw['w_pg']), full(lw['w_pe'])],
        out_specs=row(D_MODEL), out_shape=jax.ShapeDtypeStruct((n, D_MODEL), F32),
        scratch_shapes=[pltpu.VMEM((tm, D_MODEL), BF16), pltpu.VMEM((tm, D_MODEL), F32)], name='back_ffn',
        compiler_params=pltpu.CompilerParams(dimension_semantics=('arbitrary', 'arbitrary'),
                                             vmem_limit_bytes=VMEM_LIMIT),
    )(x2d, p2d, lw['g_ffn'], lw['w_up'], lw['w_down'], lw['g_ple'], lw['w_pg'], lw['w_pe'])


def _page_copies(pt_ref, first, srcs, bufs, sems, slot, npg):
    copies = []
    for pg in range(npg):
        page = pt_ref[first + pg]
        for c, (src, buf) in enumerate(zip(srcs, bufs)):
            copies.append(pltpu.make_async_copy(src.at[page], buf.at[slot, pg], sems.at[slot, c]))
    return copies


def _fetch_pages(pt_ref, srcs, bufs, sems, npg):
    n = pl.program_id(0) * pl.num_programs(1) + pl.program_id(1)
    total = pl.num_programs(0) * pl.num_programs(1)
    slot = n % 2

    @pl.when(n == 0)
    def _():
        for cp in _page_copies(pt_ref, 0, srcs, bufs, sems, 0, npg):
            cp.start()

    for cp in _page_copies(pt_ref, n * npg, srcs, bufs, sems, slot, npg):
        cp.wait()

    @pl.when(n + 1 < total)
    def _():
        for cp in _page_copies(pt_ref, (n + 1) * npg, srcs, bufs, sems, 1 - slot, npg):
            cp.start()
    return slot


def _mla_decode_kernel(pt_ref, q_ref, iq_ref, iw_ref, cn_ref, krn_ref, kinvn_ref, ikn_ref,
                       wukt_ref, wuv_ref, ckv_hbm, krp_hbm, kix_hbm,
                       o_ref, keys_ref, keysn_ref, m_ref, l_ref, acc_ref, s_ref, c16_ref,
                       ckv_buf, krp_buf, kix_buf, sems, *, npg, cpg, t):
    slot = _fetch_pages(pt_ref, (ckv_hbm, krp_hbm, kix_hbm), (ckv_buf, krp_buf, kix_buf), sems, npg)
    ckv = [ckv_buf.at[slot, pg] for pg in range(npg)]
    krp = [krp_buf.at[slot, pg] for pg in range(npg)]
    kix = [kix_buf.at[slot, pg] for pg in range(npg)]
    g = pl.program_id(1)
    rows = H_A * t
    nup = H_A * DN_A

    @pl.when(g == 0)
    def _():
        m_ref[...] = jnp.full(m_ref.shape, NEG, F32)
        l_ref[...] = jnp.zeros(l_ref.shape, F32)
        acc_ref[...] = jnp.zeros(acc_ref.shape, F32)

    q = q_ref[0]
    qlat, qr = q[:, :R_KV], q[:, R_KV:R_KV + DR_A]
    lhs = jnp.concatenate([wukt_ref[...], qlat], axis=0)
    iq = iq_ref[0]
    iw = iw_ref[0]

    def softmax_step(s, v16):
        m_prev = m_ref[...]
        m_new = jnp.maximum(m_prev, jnp.max(s, axis=-1, keepdims=True))
        p = jnp.exp(s - m_new)
        alpha = jnp.exp(m_prev - m_new)
        l_ref[...] = alpha * l_ref[...] + jnp.sum(p, axis=-1, keepdims=True)
        acc_ref[...] = alpha * acc_ref[...] + _dot(p.astype(BF16), v16)
        m_ref[...] = m_new

    def per_head(s, kinv_rows):
        return jnp.concatenate([s[h * t:(h + 1) * t, :] * kinv_rows[h] for h in range(H_A)], axis=0)

    def combine_heads(d):
        sc = jnp.zeros((t, d.shape[1]), F32)
        for h in range(HI_B):
            sc = sc + d[h * t:(h + 1) * t, :] * iw[h * t:(h + 1) * t, :1]
        return sc

    nch = npg // cpg
    chunk_cols = [slice(ch * cpg * PAGE, (ch + 1) * cpg * PAGE) for ch in range(nch)]

    def products(ch):
        pages = range(ch * cpg, (ch + 1) * cpg)
        c16 = jnp.concatenate([ckv[pg][...].astype(BF16) for pg in pages], axis=0)
        c16_ref[chunk_cols[ch], :] = c16
        krt = jnp.concatenate([krp[pg][...] for pg in pages], axis=1)
        ikt = jnp.concatenate([kix[pg][...].astype(BF16) for pg in pages], axis=1)
        r = _dot_nt(lhs, c16)
        return r, _dot(qr, krt.astype(BF16)), jnp.sum(krt * krt, axis=0, keepdims=True), _dot(iq, ikt)

    nxt = products(0)
    for ch in range(nch):
        r, s_rope, kr2, d = nxt
        if ch + 1 < nch:
            nxt = products(ch + 1)
        kinv = []
        for h in range(H_A):
            knh = r[h * DN_A:(h + 1) * DN_A, :]
            ms = jnp.sum(knh * knh, axis=0, keepdims=True) + kr2
            kinv.append(lax.rsqrt(ms * (1.0 / (DN_A + DR_A)) + EPS))
        s_ref[:, chunk_cols[ch]] = per_head(r[nup:, :] + s_rope, kinv)
        keys_ref[0, :, chunk_cols[ch]] = _score_key(combine_heads(jnp.maximum(d, 0.0)))
    softmax_step(s_ref[...], c16_ref[...])

    @pl.when(g == pl.num_programs(1) - 1)
    def _():
        pad = lambda a: jnp.concatenate([a, jnp.zeros((PAGE - t, a.shape[1]), a.dtype)], axis=0)
        col = lax.broadcasted_iota(jnp.int32, (rows, PAGE), 1)
        qt = lax.broadcasted_iota(jnp.int32, (rows, PAGE), 0) % t
        cn16 = pad(cn_ref[0]).astype(BF16)
        kinvn = kinvn_ref[0]
        s = _dot_nt(qlat, cn16) + _dot_nt(qr, pad(krn_ref[0]).astype(BF16))
        s = per_head(s, [kinvn[h:h + 1, :] for h in range(H_A)])
        softmax_step(jnp.where(col <= qt, s, NEG), cn16)
        dn = jnp.maximum(_dot_nt(iq, pad(ikn_ref[0]).astype(BF16)), 0.0)
        scn = jnp.where((col <= qt)[:t], combine_heads(dn), -jnp.inf)
        keysn_ref[0] = _score_key(scn)
        o = acc_ref[...] / l_ref[...]
        olat = jnp.concatenate([o[h * t:(h + 1) * t, :] for h in range(H_A)], axis=1).astype(BF16)
        o_ref[0] = _dot(olat, wuv_ref[...]).astype(BF16)


def _mla_decode(pt_flat, qs, iqs, iws, cn, krn, kinvn, ikn, lw, cache_ckv, krope_t, kidx_t, b, t, n_pages):
    npg = min(DECODE_PAGES, n_pages)
    cpg = min(4, npg)
    ng = n_pages // npg
    past = n_pages * PAGE
    per_b = lambda a: pl.BlockSpec((1,) + a.shape[1:], lambda bb, g, pt: (bb,) + (0,) * (a.ndim - 1))
    full = lambda a: pl.BlockSpec(a.shape, lambda bb, g, pt: (0,) * a.ndim)

    small = [qs, iqs, iws, cn, krn, kinvn, ikn]
    consts = [lw['wukt'], lw['wuv_bd']]
    in_specs = ([per_b(a) for a in small] + [full(a) for a in consts]
                + [pl.BlockSpec(memory_space=pl.ANY)] * 3)
    grid_spec = pltpu.PrefetchScalarGridSpec(
        num_scalar_prefetch=1, grid=(b, ng), in_specs=in_specs,
        out_specs=[pl.BlockSpec((1, t, 512), lambda bb, g, pt: (bb, 0, 0)),
                   pl.BlockSpec((1, t, npg * PAGE), lambda bb, g, pt: (bb, 0, g)),
                   pl.BlockSpec((1, t, PAGE), lambda bb, g, pt: (bb, 0, 0))],
        scratch_shapes=[pltpu.VMEM((H_A * t, 1), F32), pltpu.VMEM((H_A * t, 1), F32),
                        pltpu.VMEM((H_A * t, R_KV), F32), pltpu.VMEM((H_A * t, npg * PAGE), F32),
                        pltpu.VMEM((npg * PAGE, R_KV), BF16),
                        pltpu.VMEM((2, npg, PAGE, R_KV), F32), pltpu.VMEM((2, npg, DR_A, PAGE), F32),
                        pltpu.VMEM((2, npg, DI_B, PAGE), F32), pltpu.SemaphoreType.DMA((2, 3))])
    return pl.pallas_call(
        functools.partial(_mla_decode_kernel, npg=npg, cpg=cpg, t=t), grid_spec=grid_spec,
        out_shape=[jax.ShapeDtypeStruct((b, t, 512), BF16), jax.ShapeDtypeStruct((b, t, past), jnp.int32),
                   jax.ShapeDtypeStruct((b, t, PAGE), jnp.int32)], name='mla_decode',
        compiler_params=pltpu.CompilerParams(dimension_semantics=('arbitrary', 'arbitrary'),
                                             vmem_limit_bytes=VMEM_LIMIT),
    )(pt_flat, *small, *consts, cache_ckv, krope_t, kidx_t)


def _threshold_kernel(keys_ref, keysn_ref, kth_ref, need_ref, tie_ref, *, n_sel):
    nb, t, past = keys_ref.shape
    keys = keys_ref[...].reshape(nb * t, past)
    keysn = keysn_ref[...].reshape(nb * t, PAGE)

    def count(pred):
        return (jnp.sum(jnp.where(pred(keys), 1.0, 0.0), axis=-1, keepdims=True)
                + jnp.sum(jnp.where(pred(keysn), 1.0, 0.0), axis=-1, keepdims=True))
    kth = _kth_largest(lambda cand: count(lambda x: x >= cand), float(n_sel), (nb * t, 1))
    need = float(n_sel) - count(lambda x: x > kth)
    tied = count(lambda x: x == kth)
    wide = lambda a: jnp.broadcast_to(a, (nb * t, LANE)).reshape(nb, t, LANE)
    kth_ref[...] = wide(kth)
    need_ref[...] = wide(need)
    tie_ref[...] = wide(jnp.where(tied > need, 1.0, 0.0))


def _threshold(keys, keysn, n_sel):
    b, t, past = keys.shape
    nb = min(8, b)
    blk = lambda w: pl.BlockSpec((nb, t, w), lambda i: (i, 0, 0))
    return pl.pallas_call(
        functools.partial(_threshold_kernel, n_sel=n_sel), grid=(b // nb,),
        in_specs=[blk(past), blk(PAGE)], out_specs=[blk(LANE)] * 3,
        out_shape=[jax.ShapeDtypeStruct((b, t, LANE), jnp.int32), jax.ShapeDtypeStruct((b, t, LANE), F32),
                   jax.ShapeDtypeStruct((b, t, LANE), F32)], name='topk_threshold',
        compiler_params=pltpu.CompilerParams(dimension_semantics=('arbitrary',), vmem_limit_bytes=VMEM_LIMIT),
    )(keys, keysn)


def _dsa_decode_kernel(pt_ref, tie_ref, keys_ref, keysn_ref, kth_ref, need_ref, bq_ref, kn_ref, vn_ref,
                       k_hbm, v_hbm, o_ref, carry_ref, bias_ref, m_ref, l_ref, acc_ref, k_buf, v_buf, sems,
                       *, npg, t):
    slot = _fetch_pages(pt_ref, (k_hbm, v_hbm), (k_buf, v_buf), sems, npg)
    kp = [k_buf.at[slot, pg] for pg in range(npg)]
    vp = [v_buf.at[slot, pg] for pg in range(npg)]
    bb, g = pl.program_id(0), pl.program_id(1)
    nk = npg * PAGE
    gsz = H_B // KVH_B

    @pl.when(g == 0)
    def _():
        carry_ref[...] = jnp.zeros(carry_ref.shape, F32)
        m_ref[...] = jnp.full(m_ref.shape, NEG, F32)
        l_ref[...] = jnp.zeros(l_ref.shape, F32)
        acc_ref[...] = jnp.zeros(acc_ref.shape, F32)

    q = bq_ref[0]
    kth = kth_ref[0][:, :1]
    need = need_ref[0][:, :1]
    ties_matter = tie_ref[bb] != 0

    def attend(keys_blk, extra, scores, values):
        w = keys_blk.shape[1]

        visible = True if extra is None else extra

        @pl.when(ties_matter)
        def _():
            sel, carry = _select(keys_blk, kth, need, carry_ref[...])
            carry_ref[...] = carry
            bias_ref[:, :w] = jnp.where(sel & visible, 0.0, NEG)

        @pl.when(jnp.logical_not(ties_matter))
        def _():
            bias_ref[:, :w] = jnp.where((keys_blk >= kth) & visible, 0.0, NEG)

        rs = [slice(n * gsz * t, (n + 1) * gsz * t) for n in range(KVH_B)]
        s = jnp.concatenate([scores(n, q[rs[n], :]) for n in range(KVH_B)], axis=0)
        s = s + jnp.concatenate([bias_ref[:, :w]] * H_B, axis=0)
        m_prev = m_ref[...]
        m_new = jnp.maximum(m_prev, jnp.max(s, axis=-1, keepdims=True))
        p = jnp.exp(s - m_new)
        alpha = jnp.exp(m_prev - m_new)
        l_ref[...] = alpha * l_ref[...] + jnp.sum(p, axis=-1, keepdims=True)
        pv = jnp.concatenate([values(n, p[rs[n], :].astype(BF16)) for n in range(KVH_B)], axis=0)
        acc_ref[...] = alpha * acc_ref[...] + pv
        m_ref[...] = m_new

    kt = [jnp.concatenate([kp[pg][n].astype(BF16) for pg in range(npg)], axis=1) for n in range(KVH_B)]
    vt = [jnp.concatenate([vp[pg][n].astype(BF16) for pg in range(npg)], axis=1) for n in range(KVH_B)]
    off = pl.multiple_of(g * nk, nk)
    attend(keys_ref[0, :, pl.ds(off, nk)], None,
           lambda n, qn: _dot(qn, kt[n]), lambda n, p16: _dot_nt(p16, vt[n]))

    @pl.when(g == pl.num_programs(1) - 1)
    def _():
        pad = lambda a: jnp.concatenate([a, jnp.zeros((PAGE - t, a.shape[1]), a.dtype)], axis=0)
        col = lax.broadcasted_iota(jnp.int32, (t, PAGE), 1)
        qt = lax.broadcasted_iota(jnp.int32, (t, PAGE), 0)
        k16, v16 = pad(kn_ref[0]).astype(BF16), pad(vn_ref[0]).astype(BF16)
        attend(keysn_ref[0], col <= qt,
               lambda n, qn: _dot_nt(qn, k16[:, n * HD_B:(n + 1) * HD_B]),
               lambda n, p16: _dot(p16, v16[:, n * HD_B:(n + 1) * HD_B]))
        o = acc_ref[...] / l_ref[...]
        o_ref[0] = jnp.concatenate([o[h * t:(h + 1) * t, :] for h in range(H_B)], axis=1).astype(BF16)


def _dsa_decode(pt_flat, keys, keysn, bqs, kn, vn, k_t, v_t, b, t, n_pages):
    npg = min(DECODE_PAGES, n_pages)
    ng = n_pages // npg
    past = n_pages * PAGE
    kth, need, tie = _threshold(keys, keysn, min(TOPK_MAX, (past + t) // 4))
    tie_flag = (jnp.max(tie, axis=(1, 2)) > 0.0).astype(jnp.int32)
    per_b = lambda a: pl.BlockSpec((1,) + a.shape[1:], lambda bb, g, pt, tf: (bb,) + (0,) * (a.ndim - 1))

    small = [keys, keysn, kth, need, bqs, kn, vn]
    grid_spec = pltpu.PrefetchScalarGridSpec(
        num_scalar_prefetch=2, grid=(b, ng),
        in_specs=[per_b(a) for a in small] + [pl.BlockSpec(memory_space=pl.ANY)] * 2,
        out_specs=pl.BlockSpec((1, t, 512), lambda bb, g, pt, tf: (bb, 0, 0)),
        scratch_shapes=[pltpu.VMEM((t, 1), F32), pltpu.VMEM((t, npg * PAGE), F32),
                        pltpu.VMEM((H_B * t, 1), F32), pltpu.VMEM((H_B * t, 1), F32),
                        pltpu.VMEM((H_B * t, HD_B), F32),
                        pltpu.VMEM((2, npg, KVH_B, HD_B, PAGE), F32), pltpu.VMEM((2, npg, KVH_B, HD_B, PAGE), F32),
                        pltpu.SemaphoreType.DMA((2, 2))])
    return pl.pallas_call(
        functools.partial(_dsa_decode_kernel, npg=npg, t=t), grid_spec=grid_spec,
        out_shape=jax.ShapeDtypeStruct((b, t, 512), BF16), name='dsa_decode',
        compiler_params=pltpu.CompilerParams(dimension_semantics=('arbitrary', 'arbitrary'),
                                             vmem_limit_bytes=VMEM_LIMIT),
    )(pt_flat, tie_flag, *small, k_t, v_t)


def _indicator(width, seg):
    e = np.zeros((width, LANE), np.float32)
    e[np.arange(width), np.arange(width) // seg] = 1.0
    return e


def _prep_layer(w, l):
    w_in_t = jnp.swapaxes(w['w_in'], 1, 2)[l]
    o_b, o_c, o_g = N_A, N_A + N_B, N_A + N_B + N_C
    w_uq = w['w_uq'][l].reshape(R_Q, H_A, DN_A + DR_A)
    wq_r = jnp.pad(w_uq[:, :, DN_A:], ((0, 0), (0, 0), (0, LANE - DR_A))).reshape(R_Q, H_A * LANE)
    g_q_a = w['g_q_a'][l]
    w_uk = w['w_uk'][l]
    eye = jnp.eye(H_A, dtype=F32)
    wabs = (eye[:, None, :, None] * jnp.transpose(w_uk, (1, 2, 0))[:, :, None, :]).reshape(H_A * DN_A, H_A * R_KV)
    wuv_bd = (eye[:, None, :, None] * jnp.transpose(w['w_uv'][l], (1, 0, 2))[:, :, None, :]
              ).reshape(H_A * R_KV, H_A * DV_A)
    e64 = _indicator(512, 64)
    e128 = _indicator(1024, 128)
    row = lambda a: a.reshape(1, -1)
    nq_b = H_B * HD_B
    padr = lambda a, n: jnp.pad(a, ((0, n - a.shape[0]), (0, 0))).astype(BF16)
    wbq = w_in_t[o_b:o_b + nq_b].reshape(H_B, HD_B, D_MODEL)[np.array(QHEAD_ORDER)].reshape(nq_b, D_MODEL)
    wb = jnp.concatenate([wbq, w_in_t[o_b + nq_b:o_c]], axis=0)
    return dict(
        g_attn=row(w['g_attn'][l]),
        wa=padr(w_in_t[:o_b], W_A), wb=padr(wb, W_B), wc=padr(w_in_t[o_c:o_g], W_C), wg=w_in_t[o_g:].astype(BF16),
        g_qa=row(w['g_qa'][l]),
        wq2=jnp.concatenate([w_uq[:, :, :DN_A].reshape(R_Q, H_A * DN_A), wq_r], axis=1).astype(BF16),
        gn=row(jnp.tile(g_q_a[:DN_A], H_A)),
        gr=row(jnp.tile(jnp.pad(g_q_a[DN_A:], (0, LANE - DR_A)), H_A)),
        g_kva=row(w['g_kva'][l]),
        wuk=w_uk.reshape(R_KV, H_A * DN_A).astype(BF16), wukt=w_uk.reshape(R_KV, H_A * DN_A).T.astype(BF16),
        wabs=(wabs * SCALE_A).astype(BF16), wuv_bd=wuv_bd.astype(BF16),
        gqb=row(jnp.tile(w['g_q_b'][l], H_B)), gkb=row(jnp.tile(w['g_k_b'][l], KVH_B)),
        e64=jnp.asarray(e64, BF16), e64t=jnp.asarray(e64.T, BF16),
        e128=jnp.asarray(e128, BF16), e128t=jnp.asarray(e128.T, BF16),
        e64x=jnp.asarray(np.kron(np.eye(H_A, dtype=np.float32), np.ones((DN_A, LANE), np.float32)), BF16),
        wa2=jnp.pad(w['w_a2'][l], ((0, LANE - R_GATE_C), (0, 0))).astype(BF16), b_a=row(w['b_a'][l]),
        goc=row(jnp.tile(w['g_o_c'][l], H_C)),
        w_pa=w['w_pa'][l].astype(BF16), w_pb=w['w_pb'][l].astype(BF16), w_pc=w['w_pc'][l].astype(BF16),
        w_o=w['w_o'][l].astype(BF16), g_ffn=row(w['g_ffn'][l]),
        w_up=w['w_up'][l].astype(BF16), w_down=w['w_down'][l].astype(BF16),
        g_ple=row(w['g_ple'][l]), w_pe=w['w_pe'][l].astype(BF16), w_pg=w['w_pg'][l].astype(BF16),
    )


def _rope_tables(pos):
    def tab(rot, period):
        inv = ROPE_THETA ** (-jnp.arange(0, rot, 2, dtype=F32) / rot)
        ang = pos.astype(F32)[:, None] * inv[None, :]
        c, s = jnp.cos(ang), jnp.sin(ang)
        npass = period - rot
        cos = jnp.concatenate([c, c, jnp.ones((pos.shape[0], npass), F32)], axis=1)
        sin = jnp.concatenate([-s, s, jnp.zeros((pos.shape[0], npass), F32)], axis=1)
        return jnp.tile(cos, (1, LANE // period)), jnp.tile(sin, (1, LANE // period))
    ca, sa = tab(DR_A, LANE)
    cb, sb = tab(ROT_B, HD_B)
    return ca, sa, cb, sb


def _head_major(a, b, t, nh, d):
    return a.reshape(b, t, nh, d).transpose(0, 2, 1, 3).reshape(b, nh * t, d)


def _gla_state_out(st, b):
    return st.reshape(b, DV_C, H_C, DK_C).transpose(0, 2, 3, 1)


def _layer(xp, xs, pp, ps, lw, l, caches, s_prev, pt_flat, tabs_p, tabs_s, dims):
    bp, tp, bs, ts, n_pages = dims
    cache_ckv, krope_t, k_t, v_t, kidx_t = caches
    np_, ns = bp * tp, bs * ts
    tm_p, tm_s = min(256, np_), min(256, ns)

    (q, kcat, c, kr, kinv, ct, bq, bk, bkb, bv, bvt, iq, ik, ikb, iwt) = _front_ab(xp, tabs_p, lw, tm_p)
    cq, ck, cv, la, cr, gates = _front_cg(xp, lw, tm_p)
    oa = _mla_prompt(q, kcat, ct, kinv, lw['wuv_bd'], bp, tp)
    ob = _dsa_prompt(bq, iq, iwt, ikb, bkb, bvt, bp, tp)
    oc, st = _gla(cq, ck, cv, la, jnp.zeros((bp, DV_C, H_C * DK_C), F32), bp, tp)
    x1 = _back_merge(xp, oa, ob, oc, cr, gates, lw, tm_p)
    yp = _back_ffn(x1, pp, lw, min(512, np_))
    rows_p = (c.reshape(bp, tp, R_KV), kr.reshape(bp, tp, DR_A), bk.reshape(bp, tp, KVH_B, HD_B),
              bv.reshape(bp, tp, KVH_B, HD_B), ik.reshape(bp, tp, DI_B), _gla_state_out(st, bp))

    (q, kcat, c, kr, kinv, ct, bq, bk, bkb, bv, bvt, iq, ik, ikb, iwt) = _front_ab(xs, tabs_s, lw, tm_s)
    cq, ck, cv, la, cr, gates = _front_cg(xs, lw, tm_s)
    qs = _head_major(q, bs, ts, H_A, 256)
    iqs = _head_major(iq, bs, ts, HI_B, DI_B)
    iws = jnp.broadcast_to(iwt[:HI_B].reshape(HI_B, bs, ts).transpose(1, 0, 2).reshape(bs, HI_B * ts, 1),
                           (bs, HI_B * ts, LANE))
    kinvn = jnp.pad(kinv[:, ::LANE].reshape(bs, ts, H_A).transpose(0, 2, 1), ((0, 0), (0, 0), (0, PAGE - ts)))
    oa, keys, keysn = _mla_decode(pt_flat, qs, iqs, iws, c.reshape(bs, ts, R_KV), kr.reshape(bs, ts, DR_A),
                                  kinvn, ik.reshape(bs, ts, DI_B), lw, cache_ckv, krope_t, kidx_t,
                                  bs, ts, n_pages)
    std_order = [QHEAD_ORDER.index(h) for h in range(H_B)]
    bqs = bq.reshape(bs, ts, H_B, HD_B)[:, :, std_order, :].transpose(0, 2, 1, 3).reshape(bs, H_B * ts, HD_B)
    ob = _dsa_decode(pt_flat, keys, keysn, bqs, bk.reshape(bs, ts, 128), bv.reshape(bs, ts, 128),
                     k_t, v_t, bs, ts, n_pages)
    s0t = s_prev.transpose(0, 3, 1, 2).reshape(bs, DV_C, H_C * DK_C)
    oc, st = _gla(cq, ck, cv, la, s0t, bs, ts)
    x1 = _back_merge(xs, oa.reshape(ns, 512), ob.reshape(ns, 512), oc, cr, gates, lw, tm_s)
    ys = _back_ffn(x1, ps, lw, min(512, ns))
    rows_s = (c.reshape(bs, ts, R_KV), kr.reshape(bs, ts, DR_A), bk.reshape(bs, ts, KVH_B, HD_B),
              bv.reshape(bs, ts, KVH_B, HD_B), ik.reshape(bs, ts, DI_B), _gla_state_out(st, bs))
    return yp, ys, rows_p, rows_s


def kernel(x_prompt, x_sample, cache_ckv, cache_krope, cache_k, cache_v, cache_kidx, state_gla, page_table,
           p_prompt, p_sample, g_attn, w_in, g_qa, w_uq, g_q_a, g_kva, w_uk, w_uv, g_q_b, g_k_b, w_a2, b_a,
           g_o_c, w_pa, w_pb, w_pc, w_o, g_ffn, w_up, w_down, g_ple, w_pe, w_pg):
    w = dict(g_attn=g_attn, w_in=w_in, g_qa=g_qa, w_uq=w_uq, g_q_a=g_q_a, g_kva=g_kva, w_uk=w_uk, w_uv=w_uv,
             g_q_b=g_q_b, g_k_b=g_k_b, w_a2=w_a2, b_a=b_a, g_o_c=g_o_c, w_pa=w_pa, w_pb=w_pb, w_pc=w_pc,
             w_o=w_o, g_ffn=g_ffn, w_up=w_up, w_down=w_down, g_ple=g_ple, w_pe=w_pe, w_pg=w_pg)
    depth = w_in.shape[0]
    bp, tp, _ = x_prompt.shape
    bs, ts, _ = x_sample.shape
    n_pages = page_table.shape[1]
    past = n_pages * PAGE
    np_, ns = bp * tp, bs * ts
    tabs_p = _rope_tables(jnp.arange(tp))
    tm_s = min(256, ns)
    tabs_s = _rope_tables(past + (jnp.arange(tm_s) % ts))
    pt_flat = page_table.reshape(-1)
    xp, xs = x_prompt.reshape(np_, D_MODEL), x_sample.reshape(ns, D_MODEL)
    pages = lambda a: a.reshape((-1,) + a.shape[2:])
    caches = (pages(cache_ckv), pages(jnp.swapaxes(cache_krope, 2, 3)),
              pages(jnp.transpose(cache_k, (0, 1, 3, 4, 2))), pages(jnp.transpose(cache_v, (0, 1, 3, 4, 2))),
              pages(jnp.swapaxes(cache_kidx, 2, 3)))
    n_phys = cache_ckv.shape[1]
    new_p, new_s = [], []
    for l in range(depth):
        lw = _prep_layer(w, l)
        xp, xs, rows_p, rows_s = _layer(
            xp, xs, p_prompt[l].reshape(np_, D_PLE), p_sample[l].reshape(ns, D_PLE), lw, l,
            caches, state_gla[l], pt_flat + l * n_phys, tabs_p, tabs_s,
            (bp, tp, bs, ts, n_pages))
        new_p.append(rows_p)
        new_s.append(rows_s)
    stack = lambda rows, i: jnp.stack([r[i] for r in rows], axis=0)
    return ((xp.reshape(bp, tp, D_MODEL), xs.reshape(bs, ts, D_MODEL))
            + tuple(stack(new_p, i) for i in range(6)) + tuple(stack(new_s, i) for i in range(6)))
```
